```python
import jax, jax.numpy as jnp
from jax import lax
import numpy as np

D_MODEL = 2048
BATCH = 2
SEQ = 4096
DEPTH = 2
DEC_BATCH = 128
DEC_SEQ = 1
PAST_LEN = 16384
PAGE_SIZE = 128

N_A_LAYERS = DEPTH // 2
N_B_LAYERS = DEPTH - N_A_LAYERS
N_DENSE_LAYERS = (DEPTH + 1) // 2
N_MOE_LAYERS = DEPTH // 2
D_RNN = D_MODEL
N_GATE_BLOCKS = 8
CONV_W = 4
RG_C = 8.0
N_HEADS = 16
Q_LORA = D_MODEL // 4
KV_LORA = 512
QK_NOPE = 128
QK_ROPE = 64
QK_HEAD = QK_NOPE + QK_ROPE
V_HEAD = 128
ROPE_THETA = 10000.0
SCALE = QK_HEAD ** -0.5
Q_BLOCK = 128
D_FF = 5632
N_EXPERTS = 8
TOP_K = 2
EPS = 1e-6

kernel_name = 'yoco_rglru_mla_hybrid_step'


def rms_norm(x, g):
    xf = x.astype(jnp.float32)
    y = xf * lax.rsqrt(jnp.mean(xf * xf, axis=-1, keepdims=True) + EPS)
    return (y * g.astype(jnp.float32)).astype(x.dtype)


def adaln(c, w, b, n):
    m = jnp.einsum('bd,de->be', jax.nn.silu(c), w) + b
    return [t[:, None, :] for t in jnp.split(m, n, axis=-1)]


def modulate(x, g, shift, scale):
    return rms_norm(x, g) * (1.0 + scale) + shift


def rope(x, pos):
    half = QK_ROPE // 2
    inv = ROPE_THETA ** (-jnp.arange(half, dtype=jnp.float32) / half)
    ang = pos.astype(jnp.float32)[:, None] * inv[None, :]
    shp = (pos.shape[0],) + (1,) * (x.ndim - 3) + (half,)
    cos = jnp.cos(ang).reshape(shp)
    sin = jnp.sin(ang).reshape(shp)
    x1 = x[..., :half].astype(jnp.float32)
    x2 = x[..., half:].astype(jnp.float32)
    return jnp.concatenate([x1 * cos - x2 * sin, x2 * cos + x1 * sin], axis=-1).astype(x.dtype)


def causal_conv(x, buf, w, b):
    xp = jnp.concatenate([buf.astype(x.dtype), x], axis=1)
    T = x.shape[1]
    y = b
    for k in range(CONV_W):
        y = y + xp[:, k:k + T] * w[k]
    return y, xp[:, -(CONV_W - 1):]


def block_diag(x, w, b):
    Bn, T = x.shape[:2]
    xb = x.reshape(Bn, T, N_GATE_BLOCKS, -1)
    return jnp.einsum('btnk,nkj->btnj', xb, w).reshape(Bn, T, -1) + b


def _lin_combine(left, right):
    a1, b1 = left
    a2, b2 = right
    return a1 * a2, a2 * b1 + b2


def rg_lru(x, h0, w_a, b_a, w_i, b_i, lam, first_pos):
    T = x.shape[1]
    r = jax.nn.sigmoid(block_diag(x, w_a, b_a).astype(jnp.float32))
    gi = jax.nn.sigmoid(block_diag(x, w_i, b_i).astype(jnp.float32))
    log_a = -RG_C * r * jax.nn.softplus(-lam.astype(jnp.float32))
    a = jnp.exp(log_a)
    mult = jnp.sqrt(-jnp.expm1(2.0 * log_a))
    pos = first_pos + jnp.arange(T)
    mult = jnp.where((pos == 0)[None, :, None], 1.0, mult)
    bx = mult * gi * x.astype(jnp.float32)
    a_cum, b_cum = lax.associative_scan(_lin_combine, (a, bx), axis=1)
    h = a_cum * h0.astype(jnp.float32)[:, None, :] + b_cum
    return h.astype(x.dtype), h[:, -1].astype(h0.dtype)


def recurrent_block(xn, h0, buf, P, a, first_pos):
    y = jax.nn.gelu(xn @ P['rg_w_y'][a] + P['rg_b_y'][a])
    xr = xn @ P['rg_w_x'][a] + P['rg_b_x'][a]
    xr, new_buf = causal_conv(xr, buf, P['rg_conv_w'][a], P['rg_conv_b'][a])
    h, h_last = rg_lru(xr, h0, P['rg_w_a'][a], P['rg_b_a'][a], P['rg_w_i'][a], P['rg_b_i'][a],
                       P['rg_lam'][a], first_pos)
    out = (h * y) @ P['rg_w_out'][a] + P['rg_b_out'][a]
    return out, h_last, new_buf


def mla_keys(ckv, kpe, w_uk, g_k):
    k_nope = jnp.einsum('nsl,lhd->nshd', ckv, w_uk)
    k_pe = jnp.broadcast_to(kpe[:, :, None, :], k_nope.shape[:3] + (QK_ROPE,))
    return rms_norm(jnp.concatenate([k_nope, k_pe], axis=-1), g_k)


def mla_kv(h, c, pos, P):
    shift, scale = adaln(c, P['kv_w_ada'], P['kv_b_ada'], 2)
    hn = modulate(h, P['kv_g_in'], shift, scale)
    kv = hn @ P['kv_w_dkv']
    ckv = rms_norm(kv[..., :KV_LORA], P['kv_g_ckv'])
    kpe = rope(kv[..., KV_LORA:], pos)
    k = mla_keys(ckv, kpe, P['kv_w_uk'], P['kv_g_k'])
    return ckv, kpe, k


def mla_queries(xn, pos, P, j):
    cq = rms_norm(xn @ P['mla_w_dq'][j], P['mla_g_cq'][j])
    q = jnp.einsum('btl,lhe->bthe', cq, P['mla_w_uq'][j])
    q = jnp.concatenate([q[..., :QK_NOPE], rope(q[..., QK_NOPE:], pos)], axis=-1)
    return rms_norm(q, P['mla_g_q'][j])


def mla_attend_prompt(q, k, ckv, w_uv):
    Bn, S = q.shape[:2]
    nb = S // Q_BLOCK
    qb = jnp.moveaxis(q.reshape(Bn, nb, Q_BLOCK, N_HEADS, QK_HEAD), 1, 0)
    kpos = jnp.arange(S)

    def block(args):
        q_blk, i = args
        s = jnp.einsum('bqhe,bkhe->bhqk', q_blk, k).astype(jnp.float32) * SCALE
        qpos = i * Q_BLOCK + jnp.arange(Q_BLOCK)
        s = jnp.where(kpos[None, :] <= qpos[:, None], s, -jnp.inf)
        p = jax.nn.softmax(s, axis=-1).astype(ckv.dtype)
        o_lat = jnp.einsum('bhqk,bkl->bqhl', p, ckv)
        return jnp.einsum('bqhl,lhv->bqhv', o_lat, w_uv)

    o = lax.map(block, (qb, jnp.arange(nb)))
    return jnp.moveaxis(o, 0, 1).reshape(Bn, S, N_HEADS, V_HEAD)


def _online_update(carry, s, vals):
    m, l, acc = carry
    m_new = jnp.maximum(m, jnp.max(s, axis=-1))
    corr = jnp.exp(m - m_new)
    p = jnp.exp(s - m_new[..., None])
    acc = acc * corr[..., None] + jnp.einsum('nhtp,npl->nhtl', p, vals.astype(jnp.float32))
    return m_new, l * corr + jnp.sum(p, axis=-1), acc


def mla_attend_sample(q, k_new, ckv_new, cache_ckv, cache_kpe, page_table, w_uk, g_k, w_uv):
    N, T = q.shape[:2]
    init = (jnp.full((N, N_HEADS, T), -jnp.inf, jnp.float32),
            jnp.zeros((N, N_HEADS, T), jnp.float32),
            jnp.zeros((N, N_HEADS, T, KV_LORA), jnp.float32))

    def page_step(carry, ids):
        ckv_p = cache_ckv[ids]
        k_p = mla_keys(ckv_p, cache_kpe[ids], w_uk, g_k)
        s = jnp.einsum('nthe,nphe->nhtp', q, k_p).astype(jnp.float32) * SCALE
        return _online_update(carry, s, ckv_p), None

    carry, _ = lax.scan(page_step, init, page_table.T)
    s_new = jnp.einsum('nthe,nphe->nhtp', q, k_new).astype(jnp.float32) * SCALE
    causal = jnp.tril(jnp.ones((T, T), dtype=bool))
    s_new = jnp.where(causal, s_new, -jnp.inf)
    m, l, acc = _online_update(carry, s_new, ckv_new)
    o_lat = (acc / l[..., None]).astype(q.dtype)
    return jnp.einsum('nhtl,lhv->nthv', o_lat, w_uv)


def swiglu(x, w_g, w_u, w_d):
    return (jax.nn.silu(x @ w_g) * (x @ w_u)) @ w_d


def moe_ffn(x, w_r, w_g, w_u, w_d):
    logits = (x @ w_r).astype(jnp.float32)
    top_v, top_i = lax.top_k(logits, TOP_K)
    gates = jax.nn.softmax(top_v, axis=-1)
    gate_e = jnp.einsum('btk,btke->bte', gates,
                        jax.nn.one_hot(top_i, N_EXPERTS, dtype=jnp.float32)).astype(x.dtype)
    out = jnp.zeros_like(x)
    for e in range(N_EXPERTS):
        out = out + gate_e[..., e:e + 1] * swiglu(x, w_g[e], w_u[e], w_d[e])
    return out


def _forward(x, c, pos, first_pos, h0s, conv0s, attend, P):
    h_out, conv_out = [], []
    kv = None
    for l in range(DEPTH):
        sh1, sc1, g1, sh2, sc2, g2 = adaln(c, P['w_ada'][l], P['b_ada'][l], 6)
        xn = modulate(x, P['g_norm'][l, 0], sh1, sc1)
        if l < N_A_LAYERS:
            mix, h_last, conv_last = recurrent_block(xn, h0s[l], conv0s[l], P, l, first_pos)
            h_out.append(h_last)
            conv_out.append(conv_last)
        else:
            if kv is None:
                kv = mla_kv(x, c, pos, P)
            j = l - N_A_LAYERS
            q = mla_queries(xn, pos, P, j)
            o = attend(q, kv)
            mix = o.reshape(x.shape[0], x.shape[1], N_HEADS * V_HEAD) @ P['mla_w_o'][j]
        x = x + g1 * mix
        xn = modulate(x, P['g_norm'][l, 1], sh2, sc2)
        if l % 2 == 0:
            d = l // 2
            f = swiglu(xn, P['ffn_w_gate'][d], P['ffn_w_up'][d], P['ffn_w_down'][d])
        else:
            e = l // 2
            f = moe_ffn(xn, P['moe_w_router'][e], P['moe_w_gate'][e], P['moe_w_up'][e], P['moe_w_down'][e])
        x = x + g2 * f
    return x, jnp.stack(h_out), jnp.stack(conv_out), kv[0], kv[1]


def setup_inputs(seed: int = 0) -> dict:
    key = jax.random.key(seed)
    ks = jax.random.split(key, 48)
    D = D_MODEL
    HV = N_HEADS * V_HEAD
    bw = D_RNN // N_GATE_BLOCKS
    n_pages = PAST_LEN // PAGE_SIZE
    n_phys = (DEC_BATCH * n_pages * 5) // 4

    def nrm(i, shape, scale=1.0):
        return jax.random.normal(ks[i], shape, jnp.float32) * scale

    def gain(i, shape):
        return 1.0 + nrm(i, shape, 0.02)

    page_table = jax.random.permutation(ks[6], n_phys)[:DEC_BATCH * n_pages]
    page_table = page_table.reshape(DEC_BATCH, n_pages).astype(jnp.int32)
    u = jax.random.uniform(ks[22], (N_A_LAYERS, D_RNN), jnp.float32, 0.9, 0.999)
    return {
        'x_prompt': nrm(0, (BATCH, SEQ, D)),
        'x_sample': nrm(1, (DEC_BATCH, DEC_SEQ, D)),
        'state_rglru_h': nrm(2, (N_A_LAYERS, DEC_BATCH, D_RNN), 0.5),
        'state_rglru_conv': nrm(3, (N_A_LAYERS, DEC_BATCH, CONV_W - 1, D_RNN)),
        'cache_ckv': nrm(4, (n_phys, PAGE_SIZE, KV_LORA)),
        'cache_kpe': nrm(5, (n_phys, PAGE_SIZE, QK_ROPE)),
        'page_table': page_table,
        'c_prompt': nrm(7, (BATCH, D)),
        'c_sample': nrm(8, (DEC_BATCH, D)),
        'w_ada': nrm(9, (DEPTH, D, 6 * D), 0.5 * D ** -0.5),
        'b_ada': nrm(10, (DEPTH, 6 * D), 0.02),
        'g_norm': gain(11, (DEPTH, 2, D)),
        'rg_w_y': nrm(12, (N_A_LAYERS, D, D_RNN), D ** -0.5),
        'rg_b_y': nrm(13, (N_A_LAYERS, D_RNN), 0.02),
        'rg_w_x': nrm(14, (N_A_LAYERS, D, D_RNN), D ** -0.5),
        'rg_b_x': nrm(15, (N_A_LAYERS, D_RNN), 0.02),
        'rg_conv_w': nrm(16, (N_A_LAYERS, CONV_W, D_RNN), CONV_W ** -0.5),
        'rg_conv_b': nrm(17, (N_A_LAYERS, D_RNN), 0.02),
        'rg_w_a': nrm(18, (N_A_LAYERS, N_GATE_BLOCKS, bw, bw), bw ** -0.5),
        'rg_b_a': nrm(19, (N_A_LAYERS, D_RNN), 0.02),
        'rg_w_i': nrm(20, (N_A_LAYERS, N_GATE_BLOCKS, bw, bw), bw ** -0.5),
        'rg_b_i': nrm(21, (N_A_LAYERS, D_RNN), 0.02),
        'rg_lam': jnp.log(u) - jnp.log1p(-u),
        'rg_w_out': nrm(23, (N_A_LAYERS, D_RNN, D), D_RNN ** -0.5),
        'rg_b_out': nrm(24, (N_A_LAYERS, D), 0.02),
        'kv_w_ada': nrm(25, (D, 2 * D), 0.5 * D ** -0.5),
        'kv_b_ada': nrm(26, (2 * D,), 0.02),
        'kv_g_in': gain(27, (D,)),
        'kv_w_dkv': nrm(28, (D, KV_LORA + QK_ROPE), D ** -0.5),
        'kv_g_ckv': gain(29, (KV_LORA,)),
        'kv_w_uk': nrm(30, (KV_LORA, N_HEADS, QK_NOPE), KV_LORA ** -0.5),
        'kv_w_uv': nrm(31, (KV_LORA, N_HEADS, V_HEAD), KV_LORA ** -0.5),
        'kv_g_k': gain(32, (QK_HEAD,)),
        'mla_w_dq': nrm(33, (N_B_LAYERS, D, Q_LORA), D ** -0.5),
        'mla_g_cq': gain(34, (N_B_LAYERS, Q_LORA)),
        'mla_w_uq': nrm(35, (N_B_LAYERS, Q_LORA, N_HEADS, QK_HEAD), Q_LORA ** -0.5),
        'mla_g_q': gain(36, (N_B_LAYERS, QK_HEAD)),
        'mla_w_o': nrm(37, (N_B_LAYERS, HV, D), HV ** -0.5),
        'ffn_w_gate': nrm(38, (N_DENSE_LAYERS, D, D_FF), D ** -0.5),
        'ffn_w_up': nrm(39, (N_DENSE_LAYERS, D, D_FF), D ** -0.5),
        'ffn_w_down': nrm(40, (N_DENSE_LAYERS, D_FF, D), D_FF ** -0.5),
        'moe_w_router': nrm(41, (N_MOE_LAYERS, D, N_EXPERTS), D ** -0.5),
        'moe_w_gate': nrm(42, (N_MOE_LAYERS, N_EXPERTS, D, D_FF), D ** -0.5),
        'moe_w_up': nrm(43, (N_MOE_LAYERS, N_EXPERTS, D, D_FF), D ** -0.5),
        'moe_w_down': nrm(44, (N_MOE_LAYERS, N_EXPERTS, D_FF, D), D_FF ** -0.5),
    }


def reference(x_prompt, x_sample, state_rglru_h, state_rglru_conv, cache_ckv, cache_kpe, page_table,
              c_prompt, c_sample, w_ada, b_ada, g_norm,
              rg_w_y, rg_b_y, rg_w_x, rg_b_x, rg_conv_w, rg_conv_b, rg_w_a, rg_b_a, rg_w_i, rg_b_i,
              rg_lam, rg_w_out, rg_b_out,
              kv_w_ada, kv_b_ada, kv_g_in, kv_w_dkv, kv_g_ckv, kv_w_uk, kv_w_uv, kv_g_k,
              mla_w_dq, mla_g_cq, mla_w_uq, mla_g_q, mla_w_o,
              ffn_w_gate, ffn_w_up, ffn_w_down,
              moe_w_router, moe_w_gate, moe_w_up, moe_w_down):
    P = {
        'w_ada': w_ada, 'b_ada': b_ada, 'g_norm': g_norm,
        'rg_w_y': rg_w_y, 'rg_b_y': rg_b_y, 'rg_w_x': rg_w_x, 'rg_b_x': rg_b_x,
        'rg_conv_w': rg_conv_w, 'rg_conv_b': rg_conv_b, 'rg_w_a': rg_w_a, 'rg_b_a': rg_b_a,
        'rg_w_i': rg_w_i, 'rg_b_i': rg_b_i, 'rg_lam': rg_lam, 'rg_w_out': rg_w_out, 'rg_b_out': rg_b_out,
        'kv_w_ada': kv_w_ada, 'kv_b_ada': kv_b_ada, 'kv_g_in': kv_g_in, 'kv_w_dkv': kv_w_dkv,
        'kv_g_ckv': kv_g_ckv, 'kv_w_uk': kv_w_uk, 'kv_w_uv': kv_w_uv, 'kv_g_k': kv_g_k,
        'mla_w_dq': mla_w_dq, 'mla_g_cq': mla_g_cq, 'mla_w_uq': mla_w_uq, 'mla_g_q': mla_g_q,
        'mla_w_o': mla_w_o,
        'ffn_w_gate': ffn_w_gate, 'ffn_w_up': ffn_w_up, 'ffn_w_down': ffn_w_down,
        'moe_w_router': moe_w_router, 'moe_w_gate': moe_w_gate, 'moe_w_up': moe_w_up,
        'moe_w_down': moe_w_down,
    }
    Bp, S = x_prompt.shape[:2]
    T = x_sample.shape[1]
    past = page_table.shape[1] * PAGE_SIZE

    pos_p = jnp.arange(S)
    h0_p = jnp.zeros((N_A_LAYERS, Bp, D_RNN), x_prompt.dtype)
    conv0_p = jnp.zeros((N_A_LAYERS, Bp, CONV_W - 1, D_RNN), x_prompt.dtype)
    attend_p = lambda q, kv: mla_attend_prompt(q, kv[2], kv[0], kv_w_uv)
    y_prompt, h_p, conv_p, ckv_p, kpe_p = _forward(x_prompt, c_prompt, pos_p, 0, h0_p, conv0_p, attend_p, P)

    pos_s = past + jnp.arange(T)
    attend_s = lambda q, kv: mla_attend_sample(q, kv[2], kv[0], cache_ckv, cache_kpe, page_table,
                                               kv_w_uk, kv_g_k, kv_w_uv)
    y_sample, h_s, conv_s, ckv_s, kpe_s = _forward(x_sample, c_sample, pos_s, past, state_rglru_h,
                                                   state_rglru_conv, attend_s, P)
    return (y_prompt, y_sample, h_p, conv_p, ckv_p, kpe_p, h_s, conv_s, ckv_s, kpe_s)
```

```python
import functools
import math

import jax
import jax.numpy as jnp
from jax import lax
from jax.experimental import pallas as pl
from jax.experimental.pallas import tpu as pltpu

F32 = jnp.float32
BF16 = jnp.bfloat16

N_GATE_BLOCKS = 8
CONV_W = 4
RG_C = 8.0
N_HEADS = 16
KV_LORA = 512
QK_NOPE = 128
QK_ROPE = 64
QK_HEAD = QK_NOPE + QK_ROPE
V_HEAD = 128
ROPE_THETA = 10000.0
SCALE = QK_HEAD ** -0.5
N_EXPERTS = 8
EPS = 1e-6
PAGE = 128
LANES = 128
SUBLANES = 8
VMEM_LIMIT = 56 * 1024 * 1024
NT_DIMS = (((1,), (1,)), ((), ()))


def _cp(sem):
    return pltpu.CompilerParams(dimension_semantics=sem, vmem_limit_bytes=VMEM_LIMIT)


def _tile(n, pref):
    return pref if n % pref == 0 else n


def _mm_body(*refs, act, has_bias, has_res, has_rms, pre_silu):
    x_ref, w_ref = refs[0], refs[1]
    k = 2
    b_ref = res_ref = gate_ref = g_ref = None
    if has_bias:
        b_ref = refs[k]; k += 1
    if has_res:
        res_ref, gate_ref = refs[k], refs[k + 1]; k += 2
    if has_rms:
        g_ref = refs[k]; k += 1
    o_ref, wbf = refs[k], refs[k + 1]

    @pl.when((pl.program_id(1) == 0) & (pl.program_id(2) == 0))
    def _():
        wbf[...] = w_ref[...].astype(BF16)

    x = x_ref[...]
    if pre_silu:
        x = x * jax.nn.sigmoid(x)
    acc = jnp.dot(x.astype(BF16), wbf[...], preferred_element_type=F32)
    if has_bias:
        acc = acc + b_ref[...]
    if act == "gelu":
        acc = jax.nn.gelu(acc)
    if has_rms:
        acc = acc * lax.rsqrt(jnp.mean(acc * acc, axis=-1, keepdims=True) + EPS) * g_ref[...]
    if has_res:
        acc = res_ref[...] + gate_ref[...] * acc
    o_ref[...] = acc.astype(o_ref.dtype)


def mm(x, w, widx=(), *, bias=None, act=None, res=None, gate=None, gate_chunk=0,
       rms_g=None, pre_silu=False, out_dtype=F32, tm=512, tn=512, name="mm"):
    G, T, K = x.shape
    N = w.shape[-1]
    tm = _tile(T, tm)
    tn = _tile(N, tn)
    nb = N // tn
    widx = tuple(widx)
    in_specs = [
        pl.BlockSpec((None, tm, K), lambda j, g, i: (g, i, 0)),
        pl.BlockSpec((None,) * len(widx) + (K, tn), lambda j, g, i: widx + (0, j)),
    ]
    args = [x, w]
    if bias is not None:
        in_specs.append(pl.BlockSpec((1, tn), lambda j, g, i: (0, j)))
        args.append(bias.reshape(1, N))
    if res is not None:
        R = gate.shape[1]
        in_specs.append(pl.BlockSpec((None, tm, tn), lambda j, g, i: (g, i, j)))
        in_specs.append(pl.BlockSpec((None, R, tn), lambda j, g, i: (g, 0, gate_chunk * nb + j)))
        args += [res, gate]
    if rms_g is not None:
        assert tn == N
        in_specs.append(pl.BlockSpec((1, tn), lambda j, g, i: (0, j)))
        args.append(rms_g.reshape(1, N))
    body = functools.partial(_mm_body, act=act, has_bias=bias is not None, has_res=res is not None,
                             has_rms=rms_g is not None, pre_silu=pre_silu)
    return pl.pallas_call(
        body,
        grid=(nb, G, T // tm),
        in_specs=in_specs,
        out_specs=pl.BlockSpec((None, tm, tn), lambda j, g, i: (g, i, j)),
        out_shape=jax.ShapeDtypeStruct((G, T, N), out_dtype),
        scratch_shapes=[pltpu.VMEM((K, tn), BF16)],
        compiler_params=_cp(("arbitrary", "arbitrary", "arbitrary")),
        name=name,
    )(*args)


def _swiglu_body(x_ref, wg_ref, wu_ref, o_ref, wg_bf, wu_bf):
    @pl.when((pl.program_id(1) == 0) & (pl.program_id(2) == 0))
    def _():
        wg_bf[...] = wg_ref[...].astype(BF16)
        wu_bf[...] = wu_ref[...].astype(BF16)

    x = x_ref[...]
    a = jnp.dot(x, wg_bf[...], preferred_element_type=F32)
    u = jnp.dot(x, wu_bf[...], preferred_element_type=F32)
    o_ref[...] = (a * jax.nn.sigmoid(a) * u).astype(o_ref.dtype)


def swiglu_up(x, wg, wu, widx, *, tm=512, tn=512):
    G, T, K = x.shape
    N = wg.shape[-1]
    tm = _tile(T, tm)
    tn = _tile(N, tn)
    widx = tuple(widx)
    wspec = pl.BlockSpec((None,) * len(widx) + (K, tn), lambda j, g, i: widx + (0, j))
    return pl.pallas_call(
        _swiglu_body,
        grid=(N // tn, G, T // tm),
        in_specs=[pl.BlockSpec((None, tm, K), lambda j, g, i: (g, i, 0)), wspec, wspec],
        out_specs=pl.BlockSpec((None, tm, tn), lambda j, g, i: (g, i, j)),
        out_shape=jax.ShapeDtypeStruct((G, T, N), BF16),
        scratch_shapes=[pltpu.VMEM((K, tn), BF16), pltpu.VMEM((K, tn), BF16)],
        compiler_params=_cp(("arbitrary", "arbitrary", "arbitrary")),
        name="swiglu_up",
    )(x, wg, wu)


def _modulated_norm(x, g, shift, scale):
    y = x * lax.rsqrt(jnp.mean(x * x, axis=-1, keepdims=True) + EPS)
    return (y * g) * (1.0 + scale) + shift


def _norm_body(x_ref, g_ref, sh_ref, sc_ref, o_ref):
    o_ref[...] = _modulated_norm(x_ref[...], g_ref[...], sh_ref[...], sc_ref[...]).astype(o_ref.dtype)


def _norm_router_body(x_ref, g_ref, sh_ref, sc_ref, wr_ref, o_ref, gate_ref, idx_ref):
    xn = _modulated_norm(x_ref[...], g_ref[...], sh_ref[...], sc_ref[...])
    o_ref[...] = xn.astype(o_ref.dtype)
    logits = jnp.dot(xn, wr_ref[...], preferred_element_type=F32, precision=lax.Precision.HIGHEST)
    lane_i = lax.broadcasted_iota(jnp.int32, logits.shape, 1)
    lane = lane_i.astype(F32)
    neg = jnp.float32(-jnp.inf)
    lg = jnp.where(lane_i < N_EXPERTS, logits, neg)
    m1 = jnp.max(lg, axis=-1, keepdims=True)
    i1 = jnp.min(jnp.where(lg == m1, lane, float(LANES)), axis=-1, keepdims=True)
    lg2 = jnp.where(lane == i1, neg, lg)
    m2 = jnp.max(lg2, axis=-1, keepdims=True)
    i2 = jnp.min(jnp.where(lg2 == m2, lane, float(LANES)), axis=-1, keepdims=True)
    e2 = jnp.exp(m2 - m1)
    g1 = 1.0 / (1.0 + e2)
    g2 = e2 * g1
    gate_ref[...] = jnp.where(lane_i == 0, g1, jnp.where(lane_i == 1, g2, 0.0))
    idx_ref[...] = jnp.where(lane_i == 0, i1, jnp.where(lane_i == 1, i2, 0.0)).astype(jnp.int32)


def norm_mod(x, gain, mods, k_shift, k_scale, *, router_w=None, tm=256):
    G, T, D = x.shape
    R = mods.shape[1]
    tm = _tile(T, tm)
    if R != 1:
        assert R == T and tm == T
    in_specs = [
        pl.BlockSpec((None, tm, D), lambda g, i: (g, i, 0)),
        pl.BlockSpec((1, D), lambda g, i: (0, 0)),
        pl.BlockSpec((None, R, D), lambda g, i: (g, 0, k_shift)),
        pl.BlockSpec((None, R, D), lambda g, i: (g, 0, k_scale)),
    ]
    args = [x, gain.reshape(1, D), mods, mods]
    xspec = pl.BlockSpec((None, tm, D), lambda g, i: (g, i, 0))
    if router_w is None:
        return pl.pallas_call(
            _norm_body, grid=(G, T // tm), in_specs=in_specs, out_specs=xspec,
            out_shape=jax.ShapeDtypeStruct((G, T, D), BF16),
            compiler_params=_cp(("arbitrary", "arbitrary")), name="norm_mod",
        )(*args)
    wr = jnp.zeros((D, LANES), F32).at[:, :N_EXPERTS].set(router_w)
    in_specs.append(pl.BlockSpec((D, LANES), lambda g, i: (0, 0)))
    lspec = pl.BlockSpec((None, tm, LANES), lambda g, i: (g, i, 0))
    return pl.pallas_call(
        _norm_router_body, grid=(G, T // tm), in_specs=in_specs,
        out_specs=[xspec, lspec, lspec],
        out_shape=[jax.ShapeDtypeStruct((G, T, D), BF16),
                   jax.ShapeDtypeStruct((G, T, LANES), F32),
                   jax.ShapeDtypeStruct((G, T, LANES), jnp.int32)],
        compiler_params=_cp(("arbitrary", "arbitrary")), name="norm_router",
    )(*args, wr)


def _rg_gates(xc, wa, wi, ba, bi, lam):
    xb = xc.astype(BF16)
    r = jax.nn.sigmoid(jnp.dot(xb, wa.astype(BF16), preferred_element_type=F32) + ba)
    gi = jax.nn.sigmoid(jnp.dot(xb, wi.astype(BF16), preferred_element_type=F32) + bi)
    neg_lam = -lam
    softplus = jnp.maximum(neg_lam, 0.0) + jnp.log1p(jnp.exp(-jnp.abs(neg_lam)))
    log_a = -RG_C * r * softplus
    a = jnp.exp(log_a)
    th = jnp.tanh(log_a)
    mult = jnp.sqrt(-2.0 * th / (1.0 - th))
    return a, mult, gi


def _rglru_prompt_body(xr_ref, y_ref, cw_ref, cb_ref, wa_ref, wi_ref, ba_ref, bi_ref, lam_ref,
                       hy_ref, hl_ref, ext, a_s, b_s, h_s, *, tc):
    t = pl.program_id(2)
    C = xr_ref.shape[-1]

    @pl.when(t == 0)
    def _():
        ext[0:SUBLANES, :] = jnp.zeros((SUBLANES, C), F32)
        h_s[...] = jnp.zeros((SUBLANES, C), F32)

    ext[SUBLANES:SUBLANES + tc, :] = xr_ref[...]
    xc = cb_ref[...]
    for k in range(CONV_W):
        off = SUBLANES - (CONV_W - 1) + k
        xc = xc + cw_ref[k:k + 1, :] * ext[off:off + tc, :]
    ext[0:SUBLANES, :] = ext[tc:tc + SUBLANES, :]

    a, mult, gi = _rg_gates(xc, wa_ref[...], wi_ref[...], ba_ref[...], bi_ref[...], lam_ref[...])
    pos = t * tc + lax.broadcasted_iota(jnp.int32, (tc, 1), 0)
    mult = jnp.where(pos == 0, 1.0, mult)
    a_s[...] = a
    b_s[...] = mult * gi * xc

    ridx = lax.broadcasted_iota(jnp.int32, (SUBLANES, C), 0)

    def group(gidx, h):
        r0 = pl.multiple_of(gidx * SUBLANES, SUBLANES)
        a8 = a_s[pl.ds(r0, SUBLANES), :]
        b8 = b_s[pl.ds(r0, SUBLANES), :]
        for s in (1, 2, 4):
            keep = ridx >= s
            a_sh = jnp.where(keep, pltpu.roll(a8, s, 0), 1.0)
            b_sh = jnp.where(keep, pltpu.roll(b8, s, 0), 0.0)
            b8 = a8 * b_sh + b8
            a8 = a8 * a_sh
        h8 = a8 * h + b8
        hy_ref[pl.ds(r0, SUBLANES), :] = (h8 * y_ref[pl.ds(r0, SUBLANES), :]).astype(hy_ref.dtype)
        return jnp.broadcast_to(h8[SUBLANES - 1:SUBLANES, :], h8.shape)

    h = lax.fori_loop(0, tc // SUBLANES, group, h_s[...], unroll=4)
    h_s[...] = h
    hl_ref[...] = h[0:1, :]


def rglru_prompt(xr, y, conv_w, conv_b, w_a, w_i, b_a, b_i, lam, *, tc=512):
    G, T, C = xr.shape
    bw = C // N_GATE_BLOCKS
    tc = _tile(T, tc)
    tspec = pl.BlockSpec((None, tc, bw), lambda g, n, t: (g, t, n))
    vspec = pl.BlockSpec((1, bw), lambda g, n, t: (0, n))
    wspec = pl.BlockSpec((None, bw, bw), lambda g, n, t: (n, 0, 0))
    return pl.pallas_call(
        functools.partial(_rglru_prompt_body, tc=tc),
        grid=(G, N_GATE_BLOCKS, T // tc),
        in_specs=[tspec, tspec, pl.BlockSpec((CONV_W, bw), lambda g, n, t: (0, n)), vspec,
                  wspec, wspec, vspec, vspec, vspec],
        out_specs=[tspec, pl.BlockSpec((None, 1, bw), lambda g, n, t: (g, 0, n))],
        out_shape=[jax.ShapeDtypeStruct((G, T, C), BF16), jax.ShapeDtypeStruct((G, 1, C), F32)],
        scratch_shapes=[pltpu.VMEM((tc + SUBLANES, bw), F32), pltpu.VMEM((tc, bw), F32),
                        pltpu.VMEM((tc, bw), F32), pltpu.VMEM((SUBLANES, bw), F32)],
        compiler_params=_cp(("arbitrary", "arbitrary", "arbitrary")),
        name="rglru_prompt",
    )(xr, y, conv_w, conv_b.reshape(1, C), w_a, w_i, b_a.reshape(1, C), b_i.reshape(1, C),
      lam.reshape(1, C))


def _rglru_sample_body(xr_ref, y_ref, s0_ref, s1_ref, s2_ref, h0_ref, cw_ref, cb_ref, wa_ref, wi_ref,
                       ba_ref, bi_ref, lam_ref, hy_ref, h_ref):
    xc = (cb_ref[...] + cw_ref[0:1, :] * s0_ref[...] + cw_ref[1:2, :] * s1_ref[...]
          + cw_ref[2:3, :] * s2_ref[...] + cw_ref[3:4, :] * xr_ref[...])
    a, mult, gi = _rg_gates(xc, wa_ref[...], wi_ref[...], ba_ref[...], bi_ref[...], lam_ref[...])
    h = a * h0_ref[...] + mult * gi * xc
    h_ref[...] = h
    hy_ref[...] = (h * y_ref[...]).astype(hy_ref.dtype)


def rglru_sample(xr, y, state, h0, conv_w, conv_b, w_a, w_i, b_a, b_i, lam):
    _, B, C = xr.shape
    bw = C // N_GATE_BLOCKS
    st = state.reshape(B, (CONV_W - 1) * C)
    tspec = pl.BlockSpec((None, B, bw), lambda n: (0, 0, n))
    sspec = [pl.BlockSpec((B, bw), lambda n, k=k: (0, k * N_GATE_BLOCKS + n)) for k in range(CONV_W - 1)]
    vspec = pl.BlockSpec((1, bw), lambda n: (0, n))
    wspec = pl.BlockSpec((None, bw, bw), lambda n: (n, 0, 0))
    hspec = pl.BlockSpec((B, bw), lambda n: (0, n))
    return pl.pallas_call(
        _rglru_sample_body,
        grid=(N_GATE_BLOCKS,),
        in_specs=[tspec, tspec] + sspec + [hspec, pl.BlockSpec((CONV_W, bw), lambda n: (0, n)), vspec,
                                           wspec, wspec, vspec, vspec, vspec],
        out_specs=[tspec, hspec],
        out_shape=[jax.ShapeDtypeStruct((1, B, C), BF16), jax.ShapeDtypeStruct((B, C), F32)],
        compiler_params=_cp(("arbitrary",)),
        name="rglru_sample",
    )(xr, y, st, st, st, h0, conv_w, conv_b.reshape(1, C), w_a, w_i, b_a.reshape(1, C),
      b_i.reshape(1, C), lam.reshape(1, C))


def _rope128(pe, cos, sin):
    return pe * cos + pltpu.roll(pe, QK_ROPE, 1) * sin


def _kv_body(x_ref, w_ref, g_ref, cos_ref, sin_ref, ckv_ref, kpe_ref, wbf):
    @pl.when((pl.program_id(0) == 0) & (pl.program_id(1) == 0))
    def _():
        wbf[...] = w_ref[...].astype(BF16)

    acc = jnp.dot(x_ref[...], wbf[...], preferred_element_type=F32)
    c = acc[:, :KV_LORA]
    ckv_ref[...] = c * lax.rsqrt(jnp.mean(c * c, axis=-1, keepdims=True) + EPS) * g_ref[...]
    rot = _rope128(acc[:, KV_LORA:], cos_ref[...], sin_ref[...])
    kpe_ref[...] = rot[:, :QK_ROPE]


def kv_latent(hn, w_ext, g_ckv, cos_t, sin_t, *, tm=512):
    G, T, D = hn.shape
    tm = _tile(T, tm)
    NE = w_ext.shape[-1]
    if cos_t.shape[0] == 1:
        tab = pl.BlockSpec((1, LANES), lambda g, i: (0, 0))
    else:
        tab = pl.BlockSpec((tm, LANES), lambda g, i: (i, 0))
    return pl.pallas_call(
        _kv_body, grid=(G, T // tm),
        in_specs=[pl.BlockSpec((None, tm, D), lambda g, i: (g, i, 0)),
                  pl.BlockSpec((D, NE), lambda g, i: (0, 0)),
                  pl.BlockSpec((1, KV_LORA), lambda g, i: (0, 0)), tab, tab],
        out_specs=[pl.BlockSpec((None, tm, KV_LORA), lambda g, i: (g, i, 0)),
                   pl.BlockSpec((None, tm, QK_ROPE), lambda g, i: (g, i, 0))],
        out_shape=[jax.ShapeDtypeStruct((G, T, KV_LORA), F32), jax.ShapeDtypeStruct((G, T, QK_ROPE), F32)],
        scratch_shapes=[pltpu.VMEM((D, NE), BF16)],
        compiler_params=_cp(("arbitrary", "arbitrary")), name="kv_latent",
    )(hn, w_ext, g_ckv.reshape(1, KV_LORA), cos_t, sin_t)


def _kv_heads_body(c_ref, pe_ref, wk_ref, wv_ref, gn_ref, gp_ref, k_ref, v_ref, wk_bf, wv_bf):
    @pl.when((pl.program_id(0) == 0) & (pl.program_id(1) == 0))
    def _():
        wk_bf[...] = wk_ref[...].astype(BF16)
        wv_bf[...] = wv_ref[...].astype(BF16)

    c = c_ref[...].astype(BF16)
    pe = pe_ref[...]
    sspe = jnp.sum(pe * pe, axis=-1, keepdims=True)
    for h in range(N_HEADS):
        kh = jnp.dot(c, wk_bf[:, h * QK_NOPE:(h + 1) * QK_NOPE], preferred_element_type=F32)
        ss = jnp.sum(kh * kh, axis=-1, keepdims=True) + sspe
        inv = lax.rsqrt(ss * (1.0 / QK_HEAD) + EPS)
        k_ref[h, :, 0:QK_NOPE] = (kh * inv * gn_ref[...]).astype(k_ref.dtype)
        k_ref[h, :, QK_NOPE:QK_HEAD] = (pe * inv * gp_ref[...]).astype(k_ref.dtype)
        v_ref[h] = jnp.dot(c, wv_bf[:, h * V_HEAD:(h + 1) * V_HEAD],
                           preferred_element_type=F32).astype(v_ref.dtype)


def kv_heads(ckv, kpe, w_uk, w_uv, g_k, *, tm=512):
    G, T, L = ckv.shape
    tm = _tile(T, tm)
    HN = N_HEADS * QK_NOPE
    HV = N_HEADS * V_HEAD
    return pl.pallas_call(
        _kv_heads_body, grid=(G, T // tm),
        in_specs=[pl.BlockSpec((None, tm, L), lambda g, i: (g, i, 0)),
                  pl.BlockSpec((None, tm, QK_ROPE), lambda g, i: (g, i, 0)),
                  pl.BlockSpec((L, HN), lambda g, i: (0, 0)),
                  pl.BlockSpec((L, HV), lambda g, i: (0, 0)),
                  pl.BlockSpec((1, QK_NOPE), lambda g, i: (0, 0)),
                  pl.BlockSpec((1, QK_ROPE), lambda g, i: (0, 0))],
        out_specs=[pl.BlockSpec((None, N_HEADS, tm, QK_HEAD), lambda g, i: (g, 0, i, 0)),
                   pl.BlockSpec((None, N_HEADS, tm, V_HEAD), lambda g, i: (g, 0, i, 0))],
        out_shape=[jax.ShapeDtypeStruct((G, N_HEADS, T, QK_HEAD), BF16),
                   jax.ShapeDtypeStruct((G, N_HEADS, T, V_HEAD), BF16)],
        scratch_shapes=[pltpu.VMEM((L, HN), BF16), pltpu.VMEM((L, HV), BF16)],
        compiler_params=_cp(("arbitrary", "arbitrary")), name="kv_heads",
    )(ckv, kpe, w_uk.reshape(L, HN), w_uv.reshape(L, HV), g_k[:QK_NOPE].reshape(1, QK_NOPE),
      g_k[QK_NOPE:].reshape(1, QK_ROPE))


def _q_body(cq_ref, w_ref, cos_ref, sin_ref, gn_ref, gp_ref, q_ref, wbf):
    @pl.when((pl.program_id(0) == 0) & (pl.program_id(1) == 0))
    def _():
        wbf[...] = w_ref[...].astype(BF16)

    cq = cq_ref[...]
    cos = cos_ref[...]
    sin = sin_ref[...]
    hw = QK_NOPE + LANES
    for h in range(N_HEADS):
        a = jnp.dot(cq, wbf[:, h * hw:(h + 1) * hw], preferred_element_type=F32)
        nope = a[:, :QK_NOPE]
        rot = _rope128(a[:, QK_NOPE:], cos, sin)
        ss = jnp.sum(nope * nope, axis=-1, keepdims=True) + jnp.sum(rot * rot, axis=-1, keepdims=True)
        inv = lax.rsqrt(ss * (1.0 / QK_HEAD) + EPS) * SCALE
        q_ref[h, :, 0:QK_NOPE] = (nope * inv * gn_ref[...]).astype(q_ref.dtype)
        q_ref[h, :, QK_NOPE:QK_HEAD] = (rot * inv * gp_ref[...])[:, :QK_ROPE].astype(q_ref.dtype)


def q_heads(cq, w_ext, cos_t, sin_t, g_q, out_dtype, *, tm=512):
    G, T, L = cq.shape
    tm = _tile(T, tm)
    NE = w_ext.shape[-1]
    if cos_t.shape[0] == 1:
        tab = pl.BlockSpec((1, LANES), lambda g, i: (0, 0))
    else:
        tab = pl.BlockSpec((tm, LANES), lambda g, i: (i, 0))
    gp = jnp.zeros((1, LANES), F32).at[0, :QK_ROPE].set(g_q[QK_NOPE:])
    return pl.pallas_call(
        _q_body, grid=(G, T // tm),
        in_specs=[pl.BlockSpec((None, tm, L), lambda g, i: (g, i, 0)),
                  pl.BlockSpec((L, NE), lambda g, i: (0, 0)), tab, tab,
                  pl.BlockSpec((1, QK_NOPE), lambda g, i: (0, 0)),
                  pl.BlockSpec((1, LANES), lambda g, i: (0, 0))],
        out_specs=pl.BlockSpec((None, N_HEADS, tm, QK_HEAD), lambda g, i: (g, 0, i, 0)),
        out_shape=jax.ShapeDtypeStruct((G, N_HEADS, T, QK_HEAD), out_dtype),
        scratch_shapes=[pltpu.VMEM((L, NE), BF16)],
        compiler_params=_cp(("arbitrary", "arbitrary")), name="q_heads",
    )(cq, w_ext, cos_t, sin_t, g_q[:QK_NOPE].reshape(1, QK_NOPE), gp)


def _flash_body(q_ref, k_ref, v_ref, o_ref, *, tq):
    i = pl.program_id(2)
    q = q_ref[...]

    def step(j, carry, masked):
        m, l, acc = carry
        r0 = pl.multiple_of(j * tq, tq)
        k = k_ref[pl.ds(r0, tq), :]
        v = v_ref[pl.ds(r0, tq), :]
        s = lax.dot_general(q, k, NT_DIMS, preferred_element_type=F32)
        if masked:
            row = lax.broadcasted_iota(jnp.int32, (tq, tq), 0)
            col = lax.broadcasted_iota(jnp.int32, (tq, tq), 1)
            s = jnp.where(col <= row, s, -jnp.inf)
        m_new = jnp.maximum(m, jnp.max(s, axis=-1, keepdims=True))
        p = jnp.exp(s - m_new)
        corr = jnp.exp(m - m_new)
        l = l * corr + jnp.sum(p, axis=-1, keepdims=True)
        acc = acc * corr + jnp.dot(p.astype(BF16), v, preferred_element_type=F32)
        return m_new, l, acc

    init = (jnp.full((tq, 1), -jnp.inf, F32), jnp.zeros((tq, 1), F32), jnp.zeros((tq, V_HEAD), F32))
    carry = lax.fori_loop(0, i, lambda j, c: step(j, c, False), init)
    m, l, acc = step(i, carry, True)
    o_ref[...] = (acc / l).astype(o_ref.dtype)


def flash_prompt(q, k, v, *, tq=512):
    G, H, T, E = q.shape
    tq = _tile(T, tq)
    return pl.pallas_call(
        functools.partial(_flash_body, tq=tq),
        grid=(G, H, T // tq),
        in_specs=[pl.BlockSpec((None, None, tq, E), lambda g, h, i: (g, h, i, 0)),
                  pl.BlockSpec((None, None, T, E), lambda g, h, i: (g, h, 0, 0)),
                  pl.BlockSpec((None, None, T, V_HEAD), lambda g, h, i: (g, h, 0, 0))],
        out_specs=pl.BlockSpec((None, tq, V_HEAD), lambda g, h, i: (g, i, h)),
        out_shape=jax.ShapeDtypeStruct((G, T, H * V_HEAD), BF16),
        compiler_params=_cp(("arbitrary", "arbitrary", "arbitrary")), name="flash_prompt",
    )(q, k, v)


def _absorb_body(q_ref, k_ref, w_ref, gn_ref, gp_ref, qlat_ref, qpe_ref, sn_ref):
    q = q_ref[...]
    qn = (q[:, :QK_NOPE] * gn_ref[...]).astype(BF16)
    qlat_ref[...] = lax.dot_general(qn, w_ref[...].astype(BF16), NT_DIMS,
                                    preferred_element_type=F32).astype(qlat_ref.dtype)
    qpe_ref[...] = (q[:, QK_NOPE:] * gp_ref[...]).astype(qpe_ref.dtype)
    s = jnp.sum(q.astype(BF16).astype(F32) * k_ref[...].astype(F32), axis=-1, keepdims=True)
    sn_ref[...] = jnp.broadcast_to(s, sn_ref.shape)


def absorb_queries(q, k, w_uk, g_k):
    _, H, B, E = q.shape
    L = w_uk.shape[0]
    return pl.pallas_call(
        _absorb_body, grid=(H,),
        in_specs=[pl.BlockSpec((None, None, B, E), lambda h: (0, h, 0, 0)),
                  pl.BlockSpec((None, None, B, E), lambda h: (0, h, 0, 0)),
                  pl.BlockSpec((L, QK_NOPE), lambda h: (0, h)),
                  pl.BlockSpec((1, QK_NOPE), lambda h: (0, 0)),
                  pl.BlockSpec((1, QK_ROPE), lambda h: (0, 0))],
        out_specs=[pl.BlockSpec((None, B, L), lambda h: (h, 0, 0)),
                   pl.BlockSpec((None, B, QK_ROPE), lambda h: (h, 0, 0)),
                   pl.BlockSpec((None, B, LANES), lambda h: (h, 0, 0))],
        out_shape=[jax.ShapeDtypeStruct((H, B, L), BF16), jax.ShapeDtypeStruct((H, B, QK_ROPE), BF16),
                   jax.ShapeDtypeStruct((H, B, LANES), F32)],
        compiler_params=_cp(("arbitrary",)), name="absorb_queries",
    )(q, k, w_uk.reshape(L, H * QK_NOPE), g_k[:QK_NOPE].reshape(1, QK_NOPE),
      g_k[QK_NOPE:].reshape(1, QK_ROPE))


def _paged_body(pt_ref, *refs, pp):
    ck_refs = refs[:pp]
    kp_refs = refs[pp:2 * pp]
    (qlat_ref, qpe_ref, sn_ref, cnew_ref, wt_ref, o_ref, waug, m_s, l_s, acc_s) = refs[2 * pp:]
    n = pl.program_id(0)
    p = pl.program_id(1)
    HN = N_HEADS * QK_NOPE

    @pl.when((n == 0) & (p == 0))
    def _():
        waug[0:HN, :] = wt_ref[...]

    @pl.when(p == 0)
    def _():
        waug[HN:HN + N_HEADS, :] = qlat_ref[...]
        m_s[...] = sn_ref[...]
        l_s[...] = jnp.ones(l_s.shape, F32)
        acc_s[...] = jnp.broadcast_to(cnew_ref[...], acc_s.shape)

    ones = jnp.ones((N_HEADS, QK_ROPE), BF16)
    qpe = qpe_ref[...]
    for c in range(pp // 2):
        cc = jnp.concatenate([ck_refs[2 * c][...].astype(BF16), ck_refs[2 * c + 1][...].astype(BF16)], axis=0)
        kp = jnp.concatenate([kp_refs[2 * c][...], kp_refs[2 * c + 1][...]], axis=0)
        P = cc.shape[0]
        kt = lax.dot_general(waug[...], cc, NT_DIMS, preferred_element_type=F32)
        kn = kt[0:HN, :].reshape(N_HEADS, QK_NOPE, P)
        ssn = jnp.sum(kn * kn, axis=1)
        s_lat = kt[HN:HN + N_HEADS, :]
        kp2 = kp * kp
        hi = kp2.astype(BF16)
        lo = (kp2 - hi.astype(F32)).astype(BF16)
        sspe = (lax.dot_general(ones, hi, NT_DIMS, preferred_element_type=F32)
                + lax.dot_general(ones, lo, NT_DIMS, preferred_element_type=F32))
        s_pe = lax.dot_general(qpe, kp.astype(BF16), NT_DIMS, preferred_element_type=F32)
        inv = lax.rsqrt((ssn + sspe) * (1.0 / QK_HEAD) + EPS)
        s = (s_lat + s_pe) * inv
        m_old = m_s[...]
        m_new = jnp.maximum(m_old, jnp.max(s, axis=-1, keepdims=True))
        pexp = jnp.exp(s - m_new[:, 0:1])
        corr = jnp.exp(m_old - m_new)
        l_s[...] = l_s[...] * corr + jnp.sum(pexp, axis=-1, keepdims=True)
        acc_s[...] = acc_s[...] * corr[:, 0:1] + jnp.dot(pexp.astype(BF16), cc, preferred_element_type=F32)
        m_s[...] = m_new

    @pl.when(p == pl.num_programs(1) - 1)
    def _():
        o_ref[...] = acc_s[...] / l_s[:, 0:1]


def paged_attention(page_table, cache_ckv, cache_kpe, qlat, qpe, sn, c_new, w_uk_t, *, pp=8):
    B, NP = page_table.shape
    L = cache_ckv.shape[-1]
    pp = pp if NP % pp == 0 else 2
    HN = N_HEADS * QK_NOPE
    ck_specs = [pl.BlockSpec((None, PAGE, L), lambda n, p, pt, k=k: (pt[n, p * pp + k], 0, 0))
                for k in range(pp)]
    kp_specs = [pl.BlockSpec((None, PAGE, QK_ROPE), lambda n, p, pt, k=k: (pt[n, p * pp + k], 0, 0))
                for k in range(pp)]
    per_seq = lambda w: pl.BlockSpec((None, N_HEADS, w), lambda n, p, pt: (n, 0, 0))
    grid_spec = pltpu.PrefetchScalarGridSpec(
        num_scalar_prefetch=1,
        grid=(B, NP // pp),
        in_specs=ck_specs + kp_specs + [
            per_seq(L), per_seq(QK_ROPE), per_seq(LANES),
            pl.BlockSpec((None, 1, L), lambda n, p, pt: (n, 0, 0)),
            pl.BlockSpec((HN, L), lambda n, p, pt: (0, 0))],
        out_specs=per_seq(L),
        scratch_shapes=[pltpu.VMEM((HN + N_HEADS, L), BF16), pltpu.VMEM((N_HEADS, LANES), F32),
                        pltpu.VMEM((N_HEADS, LANES), F32), pltpu.VMEM((N_HEADS, L), F32)],
    )
    return pl.pallas_call(
        functools.partial(_paged_body, pp=pp),
        grid_spec=grid_spec,
        out_shape=jax.ShapeDtypeStruct((B, N_HEADS, L), F32),
        compiler_params=_cp(("arbitrary", "arbitrary")), name="paged_attention",
    )(page_table, *([cache_ckv] * pp), *([cache_kpe] * pp), qlat, qpe, sn, c_new, w_uk_t)


def _head_proj_body(x_ref, w_ref, o_ref):
    o_ref[...] = jnp.dot(x_ref[...].astype(BF16), w_ref[...].astype(BF16),
                         preferred_element_type=F32).astype(o_ref.dtype)


def head_value_proj(o_lat, w_uv):
    H, B, L = o_lat.shape
    return pl.pallas_call(
        _head_proj_body, grid=(H,),
        in_specs=[pl.BlockSpec((None, B, L), lambda h: (h, 0, 0)),
                  pl.BlockSpec((L, V_HEAD), lambda h: (0, h))],
        out_specs=pl.BlockSpec((None, B, V_HEAD), lambda h: (0, 0, h)),
        out_shape=jax.ShapeDtypeStruct((1, B, H * V_HEAD), BF16),
        compiler_params=_cp(("arbitrary",)), name="head_value_proj",
    )(o_lat, w_uv)


def _new_expert(te_ref, r):
    prev = te_ref[jnp.maximum(r - 1, 0)]
    return (r == 0) | (te_ref[r] != prev)


def _moe_up_body(te_ref, tv_ref, x_ref, wg_ref, wu_ref, o_ref, wg_bf, wu_bf):
    r = pl.program_id(1)

    @pl.when(_new_expert(te_ref, r))
    def _():
        wg_bf[...] = wg_ref[...].astype(BF16)
        wu_bf[...] = wu_ref[...].astype(BF16)

    @pl.when(tv_ref[r] > 0)
    def _():
        x = x_ref[...]
        a = jnp.dot(x, wg_bf[...], preferred_element_type=F32)
        u = jnp.dot(x, wu_bf[...], preferred_element_type=F32)
        o_ref[...] = (a * jax.nn.sigmoid(a) * u).astype(o_ref.dtype)

    @pl.when(tv_ref[r] == 0)
    def _():
        o_ref[...] = jnp.zeros(o_ref.shape, o_ref.dtype)


def _moe_down_body(te_ref, tv_ref, h_ref, wd_ref, o_ref, wd_bf):
    r = pl.program_id(1)

    @pl.when(_new_expert(te_ref, r))
    def _():
        wd_bf[...] = wd_ref[...].astype(BF16)

    @pl.when(tv_ref[r] > 0)
    def _():
        o_ref[...] = jnp.dot(h_ref[...], wd_bf[...], preferred_element_type=F32)

    @pl.when(tv_ref[r] == 0)
    def _():
        o_ref[...] = jnp.zeros(o_ref.shape, o_ref.dtype)


def moe_experts(xg, tile_expert, tile_valid, w_gate, w_up, w_down, *, tm, tf=512, tn=512):
    R, D = xg.shape
    F = w_gate.shape[-1]
    tf = _tile(F, tf)
    tn = _tile(D, tn)
    up_spec = pltpu.PrefetchScalarGridSpec(
        num_scalar_prefetch=2, grid=(F // tf, R // tm),
        in_specs=[pl.BlockSpec((tm, D), lambda f, r, te, tv: (r, 0)),
                  pl.BlockSpec((None, D, tf), lambda f, r, te, tv: (te[r], 0, f)),
                  pl.BlockSpec((None, D, tf), lambda f, r, te, tv: (te[r], 0, f))],
        out_specs=pl.BlockSpec((tm, tf), lambda f, r, te, tv: (r, f)),
        scratch_shapes=[pltpu.VMEM((D, tf), BF16), pltpu.VMEM((D, tf), BF16)])
    h = pl.pallas_call(
        _moe_up_body, grid_spec=up_spec, out_shape=jax.ShapeDtypeStruct((R, F), BF16),
        compiler_params=_cp(("arbitrary", "arbitrary")), name="moe_up",
    )(tile_expert, tile_valid, xg, w_gate, w_up)
    down_spec = pltpu.PrefetchScalarGridSpec(
        num_scalar_prefetch=2, grid=(D // tn, R // tm),
        in_specs=[pl.BlockSpec((tm, F), lambda j, r, te, tv: (r, 0)),
                  pl.BlockSpec((None, F, tn), lambda j, r, te, tv: (te[r], 0, j))],
        out_specs=pl.BlockSpec((tm, tn), lambda j, r, te, tv: (r, j)),
        scratch_shapes=[pltpu.VMEM((F, tn), BF16)])
    return pl.pallas_call(
        _moe_down_body, grid_spec=down_spec, out_shape=jax.ShapeDtypeStruct((R, D), F32),
        compiler_params=_cp(("arbitrary", "arbitrary")), name="moe_down",
    )(tile_expert, tile_valid, h, w_down)


def _combine_body(x_ref, g_ref, ya_ref, yb_ref, w_ref, o_ref):
    w = w_ref[...]
    f = w[:, 0:1] * ya_ref[...] + w[:, 1:2] * yb_ref[...]
    o_ref[...] = x_ref[...] + g_ref[...] * f


def moe_combine(x, gate_mod, gate_chunk, ya, yb, w, *, tm=256):
    G, T, D = x.shape
    R = gate_mod.shape[1]
    tm = _tile(T, tm)
    tspec = pl.BlockSpec((None, tm, D), lambda g, i: (g, i, 0))
    return pl.pallas_call(
        _combine_body, grid=(G, T // tm),
        in_specs=[tspec, pl.BlockSpec((None, R, D), lambda g, i: (g, 0, gate_chunk)), tspec, tspec,
                  pl.BlockSpec((None, tm, LANES), lambda g, i: (g, i, 0))],
        out_specs=tspec, out_shape=jax.ShapeDtypeStruct((G, T, D), F32),
        compiler_params=_cp(("arbitrary", "arbitrary")), name="moe_combine",
    )(x, gate_mod, ya, yb, w)


def _moe_layout(ids, tm):
    A = ids.shape[0]
    R = -(-(A + N_EXPERTS * (tm - 1)) // tm) * tm
    onehot = (ids[:, None] == jnp.arange(N_EXPERTS, dtype=jnp.int32)[None, :]).astype(jnp.int32)
    rank = jnp.cumsum(onehot, axis=0) - onehot
    counts = jnp.sum(onehot, axis=0)
    tiles = (counts + tm - 1) // tm
    tile_end = jnp.cumsum(tiles)
    row_off = (tile_end - tiles) * tm
    pos = jnp.sum(onehot * (row_off[None, :] + rank), axis=1)
    tile_id = jnp.arange(R // tm, dtype=jnp.int32)
    tile_expert = jnp.minimum(jnp.sum((tile_id[:, None] >= tile_end[None, :]).astype(jnp.int32), axis=1),
                              N_EXPERTS - 1).astype(jnp.int32)
    tile_valid = (tile_id < tile_end[-1]).astype(jnp.int32)
    return R, pos.astype(jnp.int32), tile_expert, tile_valid


def _rope_tables(pos):
    half = QK_ROPE // 2
    inv = ROPE_THETA ** (-jnp.arange(half, dtype=F32) / half)
    ang = pos.astype(F32)[:, None] * inv[None, :]
    cos, sin = jnp.cos(ang), jnp.sin(ang)
    z = jnp.zeros((pos.shape[0], LANES - QK_ROPE), F32)
    return jnp.concatenate([cos, cos, z], axis=-1), jnp.concatenate([-sin, sin, z], axis=-1)


def _swap_halves(w):
    half = QK_ROPE // 2
    return jnp.concatenate([w[..., half:], w[..., :half]], axis=-1)


def kernel(x_prompt, x_sample, state_rglru_h, state_rglru_conv, cache_ckv, cache_kpe, page_table, c_prompt, c_sample, w_ada, b_ada, g_norm, rg_w_y, rg_b_y, rg_w_x, rg_b_x, rg_conv_w, rg_conv_b, rg_w_a, rg_b_a, rg_w_i, rg_b_i, rg_lam, rg_w_out, rg_b_out, kv_w_ada, kv_b_ada, kv_g_in, kv_w_dkv, kv_g_ckv, kv_w_uk, kv_w_uv, kv_g_k, mla_w_dq, mla_g_cq, mla_w_uq, mla_g_q, mla_w_o, ffn_w_gate, ffn_w_up, ffn_w_down, moe_w_router, moe_w_gate, moe_w_up, moe_w_down):
    Bp, S, D = x_prompt.shape
    Bs = x_sample.shape[0]
    past = page_table.shape[1] * PAGE
    HN = N_HEADS * QK_NOPE
    HV = N_HEADS * V_HEAD

    xs = {"p": x_prompt, "s": x_sample.reshape(1, Bs, D)}
    groups = ("p", "s")

    pad = (-(Bs + Bp)) % 16
    c_all = jnp.concatenate([c_sample, c_prompt, jnp.zeros((pad, D), F32)], axis=0)[None]

    def split_mods(m):
        return {"p": m[0, Bs:Bs + Bp].reshape(Bp, 1, -1), "s": m[:, :Bs]}

    mods = [split_mods(mm(c_all, w_ada, (l,), bias=b_ada[l], pre_silu=True, name="adaln"))
            for l in range(2)]
    mods_kv = split_mods(mm(c_all, kv_w_ada, bias=kv_b_ada, pre_silu=True, name="adaln_kv"))

    x1 = {}
    h_last = {}
    conv_last = {}
    for gname in groups:
        x = xs[gname]
        md = mods[0][gname]
        xn = norm_mod(x, g_norm[0, 0], md, 0, 1)
        y = mm(xn, rg_w_y, (0,), bias=rg_b_y[0], act="gelu", name="rg_y")
        xr = mm(xn, rg_w_x, (0,), bias=rg_b_x[0], name="rg_x")
        if gname == "p":
            hy, hl = rglru_prompt(xr, y, rg_conv_w[0], rg_conv_b[0], rg_w_a[0], rg_w_i[0], rg_b_a[0],
                                  rg_b_i[0], rg_lam[0])
            h_last[gname] = hl.reshape(1, Bp, D)
            conv_last[gname] = xr[:, S - (CONV_W - 1):, :][None]
        else:
            hy, hl = rglru_sample(xr, y, state_rglru_conv[0], state_rglru_h[0], rg_conv_w[0], rg_conv_b[0],
                                  rg_w_a[0], rg_w_i[0], rg_b_a[0], rg_b_i[0], rg_lam[0])
            h_last[gname] = hl[None]
            conv_last[gname] = jnp.concatenate([state_rglru_conv[0][:, 1:], xr.reshape(Bs, 1, D)], axis=1)[None]
        xa = mm(hy, rg_w_out, (0,), bias=rg_b_out[0], res=x, gate=md, gate_chunk=2, name="rg_out")
        xn = norm_mod(xa, g_norm[0, 1], md, 3, 4)
        hmid = swiglu_up(xn, ffn_w_gate, ffn_w_up, (0,))
        x1[gname] = mm(hmid, ffn_w_down, (0,), res=xa, gate=md, gate_chunk=5, tn=256, name="ffn_down")

    w_dkv_ext = jnp.concatenate([kv_w_dkv, _swap_halves(kv_w_dkv[:, KV_LORA:])], axis=-1)
    w_uq = mla_w_uq[0]
    w_uq_ext = jnp.concatenate([w_uq, _swap_halves(w_uq[..., QK_NOPE:])], axis=-1).reshape(
        w_uq.shape[0], N_HEADS * (QK_NOPE + LANES))
    tabs = {"p": _rope_tables(jnp.arange(S)), "s": _rope_tables(jnp.full((1,), past))}
    w_uk2 = kv_w_uk.reshape(KV_LORA, HN)
    w_uv2 = kv_w_uv.reshape(KV_LORA, HV)

    ckv = {}
    kpe = {}
    xb = {}
    for gname in groups:
        x = x1[gname]
        md = mods[1][gname]
        cos_t, sin_t = tabs[gname]
        hn = norm_mod(x, kv_g_in, mods_kv[gname], 0, 1)
        ckv[gname], kpe[gname] = kv_latent(hn, w_dkv_ext, kv_g_ckv, cos_t, sin_t)
        k, v = kv_heads(ckv[gname], kpe[gname], w_uk2, w_uv2, kv_g_k)
        xn = norm_mod(x, g_norm[1, 0], md, 0, 1)
        cq = mm(xn, mla_w_dq, (0,), rms_g=mla_g_cq[0], out_dtype=BF16, name="q_down")
        if gname == "p":
            q = q_heads(cq, w_uq_ext, cos_t, sin_t, mla_g_q[0], BF16)
            o = flash_prompt(q, k, v)
        else:
            q = q_heads(cq, w_uq_ext, cos_t, sin_t, mla_g_q[0], F32)
            qlat, qpe, sn = absorb_queries(q, k, w_uk2, kv_g_k)
            o_lat = paged_attention(page_table, cache_ckv, cache_kpe, jnp.swapaxes(qlat, 0, 1),
                                    jnp.swapaxes(qpe, 0, 1), jnp.swapaxes(sn, 0, 1),
                                    ckv[gname].reshape(Bs, 1, KV_LORA), w_uk2.T.astype(BF16))
            o = head_value_proj(jnp.swapaxes(o_lat, 0, 1), w_uv2)
        xb[gname] = mm(o, mla_w_o, (0,), res=x, gate=md, gate_chunk=2, name="attn_out")

    xn_l, gate_l, idx_l = [], [], []
    for gname in groups:
        xn, gt, ix = norm_mod(xb[gname], g_norm[1, 1], mods[1][gname], 3, 4, router_w=moe_w_router[0])
        xn_l.append(xn.reshape(-1, D))
        gate_l.append(gt)
        idx_l.append(ix[..., :2].reshape(-1, 2))
    xn_all = jnp.concatenate(xn_l, axis=0)
    ids = jnp.concatenate(idx_l, axis=0).reshape(-1)
    tm_moe = 256
    R, pos, tile_expert, tile_valid = _moe_layout(ids, tm_moe)
    tok_of_row = jnp.zeros((R,), jnp.int32).at[pos].set(jnp.arange(ids.shape[0], dtype=jnp.int32) // 2)
    xg = jnp.take(xn_all, tok_of_row, axis=0)
    yg = moe_experts(xg, tile_expert, tile_valid, moe_w_gate[0], moe_w_up[0], moe_w_down[0], tm=tm_moe)
    pos2 = pos.reshape(-1, 2)
    ya = jnp.take(yg, pos2[:, 0], axis=0)
    yb = jnp.take(yg, pos2[:, 1], axis=0)
    n_p = Bp * S
    out = {}
    for gi, gname in enumerate(groups):
        sl = slice(0, n_p) if gname == "p" else slice(n_p, n_p + Bs)
        shp = xb[gname].shape
        out[gname] = moe_combine(xb[gname], mods[1][gname], 5, ya[sl].reshape(shp), yb[sl].reshape(shp),
                                 gate_l[gi])

    return (out["p"], out["s"].reshape(Bs, 1, D),
            h_last["p"], conv_last["p"], ckv["p"], kpe["p"],
            h_last["s"], conv_last["s"], ckv["s"].reshape(Bs, 1, KV_LORA), kpe["s"].reshape(Bs, 1, QK_ROPE))
```

```python
import functools
import math

import jax
import jax.numpy as jnp
from jax import lax
from jax.experimental import pallas as pl
from jax.experimental.pallas import tpu as pltpu

F32 = jnp.float32
BF16 = jnp.bfloat16

N_GATE_BLOCKS = 8
CONV_W = 4
RG_C = 8.0
N_HEADS = 16
KV_LORA = 512
QK_NOPE = 128
QK_ROPE = 64
QK_HEAD = QK_NOPE + QK_ROPE
V_HEAD = 128
ROPE_THETA = 10000.0
SCALE = QK_HEAD ** -0.5
N_EXPERTS = 8
EPS = 1e-6
PAGE = 128
LANES = 128
SUBLANES = 8
VMEM_LIMIT = 56 * 1024 * 1024
NT_DIMS = (((1,), (1,)), ((), ()))


def _cp(sem):
    return pltpu.CompilerParams(dimension_semantics=sem, vmem_limit_bytes=VMEM_LIMIT)


def _tile(n, pref):
    return pref if n % pref == 0 else n


def _mm_body(*refs, act, has_bias, has_res, has_rms, pre_silu):
    x_ref, w_ref = refs[0], refs[1]
    k = 2
    b_ref = res_ref = gate_ref = g_ref = None
    if has_bias:
        b_ref = refs[k]; k += 1
    if has_res:
        res_ref, gate_ref = refs[k], refs[k + 1]; k += 2
    if has_rms:
        g_ref = refs[k]; k += 1
    o_ref, wbf = refs[k], refs[k + 1]

    @pl.when((pl.program_id(1) == 0) & (pl.program_id(2) == 0))
    def _():
        wbf[...] = w_ref[...].astype(BF16)

    x = x_ref[...]
    if pre_silu:
        x = x * jax.nn.sigmoid(x)
    acc = jnp.dot(x.astype(BF16), wbf[...], preferred_element_type=F32)
    if has_bias:
        acc = acc + b_ref[...]
    if act == "gelu":
        acc = jax.nn.gelu(acc)
    if has_rms:
        acc = acc * lax.rsqrt(jnp.mean(acc * acc, axis=-1, keepdims=True) + EPS) * g_ref[...]
    if has_res:
        acc = res_ref[...] + gate_ref[...] * acc
    o_ref[...] = acc.astype(o_ref.dtype)


def mm(x, w, widx=(), *, bias=None, act=None, res=None, gate=None, gate_chunk=0,
       rms_g=None, pre_silu=False, out_dtype=F32, tm=512, tn=512, name="mm"):
    G, T, K = x.shape
    N = w.shape[-1]
    tm = _tile(T, tm)
    tn = _tile(N, tn)
    nb = N // tn
    widx = tuple(widx)
    in_specs = [
        pl.BlockSpec((None, tm, K), lambda j, g, i: (g, i, 0)),
        pl.BlockSpec((None,) * len(widx) + (K, tn), lambda j, g, i: widx + (0, j)),
    ]
    args = [x, w]
    if bias is not None:
        in_specs.append(pl.BlockSpec((1, tn), lambda j, g, i: (0, j)))
        args.append(bias.reshape(1, N))
    if res is not None:
        R = gate.shape[1]
        in_specs.append(pl.BlockSpec((None, tm, tn), lambda j, g, i: (g, i, j)))
        in_specs.append(pl.BlockSpec((None, R, tn), lambda j, g, i: (g, 0, gate_chunk * nb + j)))
        args += [res, gate]
    if rms_g is not None:
        assert tn == N
        in_specs.append(pl.BlockSpec((1, tn), lambda j, g, i: (0, j)))
        args.append(rms_g.reshape(1, N))
    body = functools.partial(_mm_body, act=act, has_bias=bias is not None, has_res=res is not None,
                             has_rms=rms_g is not None, pre_silu=pre_silu)
    return pl.pallas_call(
        body,
        grid=(nb, G, T // tm),
        in_specs=in_specs,
        out_specs=pl.BlockSpec((None, tm, tn), lambda j, g, i: (g, i, j)),
        out_shape=jax.ShapeDtypeStruct((G, T, N), out_dtype),
        scratch_shapes=[pltpu.VMEM((K, tn), BF16)],
        compiler_params=_cp(("arbitrary", "arbitrary", "arbitrary")),
        name=name,
    )(*args)


def _swiglu_body(x_ref, wg_ref, wu_ref, o_ref, wg_bf, wu_bf):
    @pl.when((pl.program_id(1) == 0) & (pl.program_id(2) == 0))
    def _():
        wg_bf[...] = wg_ref[...].astype(BF16)
        wu_bf[...] = wu_ref[...].astype(BF16)

    x = x_ref[...]
    a = jnp.dot(x, wg_bf[...], preferred_element_type=F32)
    u = jnp.dot(x, wu_bf[...], preferred_element_type=F32)
    o_ref[...] = (a * jax.nn.sigmoid(a) * u).astype(o_ref.dtype)


def swiglu_up(x, wg, wu, widx, *, tm=512, tn=512):
    G, T, K = x.shape
    N = wg.shape[-1]
    tm = _tile(T, tm)
    tn = _tile(N, tn)
    widx = tuple(widx)
    wspec = pl.BlockSpec((None,) * len(widx) + (K, tn), lambda j, g, i: widx + (0, j))
    return pl.pallas_call(
        _swiglu_body,
        grid=(N // tn, G, T // tm),
        in_specs=[pl.BlockSpec((None, tm, K), lambda j, g, i: (g, i, 0)), wspec, wspec],
        out_specs=pl.BlockSpec((None, tm, tn), lambda j, g, i: (g, i, j)),
        out_shape=jax.ShapeDtypeStruct((G, T, N), BF16),
        scratch_shapes=[pltpu.VMEM((K, tn), BF16), pltpu.VMEM((K, tn), BF16)],
        compiler_params=_cp(("arbitrary", "arbitrary", "arbitrary")),
        name="swiglu_up",
    )(x, wg, wu)


def _modulated_norm(x, g, shift, scale):
    y = x * lax.rsqrt(jnp.mean(x * x, axis=-1, keepdims=True) + EPS)
    return (y * g) * (1.0 + scale) + shift


def _norm_body(x_ref, g_ref, sh_ref, sc_ref, o_ref):
    o_ref[...] = _modulated_norm(x_ref[...], g_ref[...], sh_ref[...], sc_ref[...]).astype(o_ref.dtype)


def _norm_router_body(x_ref, g_ref, sh_ref, sc_ref, wr_ref, o_ref, gate_ref, idx_ref):
    xn = _modulated_norm(x_ref[...], g_ref[...], sh_ref[...], sc_ref[...])
    o_ref[...] = xn.astype(o_ref.dtype)
    logits = jnp.dot(xn, wr_ref[...], preferred_element_type=F32, precision=lax.Precision.HIGHEST)
    lane_i = lax.broadcasted_iota(jnp.int32, logits.shape, 1)
    lane = lane_i.astype(F32)
    neg = jnp.float32(-jnp.inf)
    lg = jnp.where(lane_i < N_EXPERTS, logits, neg)
    m1 = jnp.max(lg, axis=-1, keepdims=True)
    i1 = jnp.min(jnp.where(lg == m1, lane, float(LANES)), axis=-1, keepdims=True)
    lg2 = jnp.where(lane == i1, neg, lg)
    m2 = jnp.max(lg2, axis=-1, keepdims=True)
    i2 = jnp.min(jnp.where(lg2 == m2, lane, float(LANES)), axis=-1, keepdims=True)
    e2 = jnp.exp(m2 - m1)
    g1 = 1.0 / (1.0 + e2)
    g2 = e2 * g1
    gate_ref[...] = jnp.where(lane_i == 0, g1, jnp.where(lane_i == 1, g2, 0.0))
    idx_ref[...] = jnp.where(lane_i == 0, i1, jnp.where(lane_i == 1, i2, 0.0)).astype(jnp.int32)


def norm_mod(x, gain, mods, k_shift, k_scale, *, router_w=None, tm=256):
    G, T, D = x.shape
    R = mods.shape[1]
    tm = _tile(T, tm)
    if R != 1:
        assert R == T and tm == T
    in_specs = [
        pl.BlockSpec((None, tm, D), lambda g, i: (g, i, 0)),
        pl.BlockSpec((1, D), lambda g, i: (0, 0)),
        pl.BlockSpec((None, R, D), lambda g, i: (g, 0, k_shift)),
        pl.BlockSpec((None, R, D), lambda g, i: (g, 0, k_scale)),
    ]
    args = [x, gain.reshape(1, D), mods, mods]
    xspec = pl.BlockSpec((None, tm, D), lambda g, i: (g, i, 0))
    if router_w is None:
        return pl.pallas_call(
            _norm_body, grid=(G, T // tm), in_specs=in_specs, out_specs=xspec,
            out_shape=jax.ShapeDtypeStruct((G, T, D), BF16),
            compiler_params=_cp(("arbitrary", "arbitrary")), name="norm_mod",
        )(*args)
    wr = jnp.zeros((D, LANES), F32).at[:, :N_EXPERTS].set(router_w)
    in_specs.append(pl.BlockSpec((D, LANES), lambda g, i: (0, 0)))
    lspec = pl.BlockSpec((None, tm, LANES), lambda g, i: (g, i, 0))
    return pl.pallas_call(
        _norm_router_body, grid=(G, T // tm), in_specs=in_specs,
        out_specs=[xspec, lspec, lspec],
        out_shape=[jax.ShapeDtypeStruct((G, T, D), BF16),
                   jax.ShapeDtypeStruct((G, T, LANES), F32),
                   jax.ShapeDtypeStruct((G, T, LANES), jnp.int32)],
        compiler_params=_cp(("arbitrary", "arbitrary")), name="norm_router",
    )(*args, wr)


def _rg_gates(xc, wa, wi, ba, bi, lam):
    xb = xc.astype(BF16)
    r = jax.nn.sigmoid(jnp.dot(xb, wa.astype(BF16), preferred_element_type=F32) + ba)
    gi = jax.nn.sigmoid(jnp.dot(xb, wi.astype(BF16), preferred_element_type=F32) + bi)
    neg_lam = -lam
    softplus = jnp.maximum(neg_lam, 0.0) + jnp.log1p(jnp.exp(-jnp.abs(neg_lam)))
    log_a = -RG_C * r * softplus
    a = jnp.exp(log_a)
    th = jnp.tanh(log_a)
    mult = jnp.sqrt(-2.0 * th / (1.0 - th))
    return a, mult, gi


def _rglru_prompt_body(xr_ref, y_ref, cw_ref, cb_ref, wa_ref, wi_ref, ba_ref, bi_ref, lam_ref,
                       hy_ref, hl_ref, ext, a_s, b_s, h_s, *, tc):
    t = pl.program_id(2)
    C = xr_ref.shape[-1]

    @pl.when(t == 0)
    def _():
        ext[0:SUBLANES, :] = jnp.zeros((SUBLANES, C), F32)
        h_s[...] = jnp.zeros((SUBLANES, C), F32)

    ext[SUBLANES:SUBLANES + tc, :] = xr_ref[...]
    xc = cb_ref[...]
    for k in range(CONV_W):
        off = SUBLANES - (CONV_W - 1) + k
        xc = xc + cw_ref[k:k + 1, :] * ext[off:off + tc, :]
    ext[0:SUBLANES, :] = ext[tc:tc + SUBLANES, :]

    a, mult, gi = _rg_gates(xc, wa_ref[...], wi_ref[...], ba_ref[...], bi_ref[...], lam_ref[...])
    pos = t * tc + lax.broadcasted_iota(jnp.int32, (tc, 1), 0)
    mult = jnp.where(pos == 0, 1.0, mult)
    a_s[...] = a
    b_s[...] = mult * gi * xc

    ridx = lax.broadcasted_iota(jnp.int32, (SUBLANES, C), 0)

    def group(gidx, h):
        r0 = pl.multiple_of(gidx * SUBLANES, SUBLANES)
        a8 = a_s[pl.ds(r0, SUBLANES), :]
        b8 = b_s[pl.ds(r0, SUBLANES), :]
        for s in (1, 2, 4):
            keep = ridx >= s
            a_sh = jnp.where(keep, pltpu.roll(a8, s, 0), 1.0)
            b_sh = jnp.where(keep, pltpu.roll(b8, s, 0), 0.0)
            b8 = a8 * b_sh + b8
            a8 = a8 * a_sh
        h8 = a8 * h + b8
        hy_ref[pl.ds(r0, SUBLANES), :] = (h8 * y_ref[pl.ds(r0, SUBLANES), :]).astype(hy_ref.dtype)
        return jnp.broadcast_to(h8[SUBLANES - 1:SUBLANES, :], h8.shape)

    h = lax.fori_loop(0, tc // SUBLANES, group, h_s[...], unroll=4)
    h_s[...] = h
    hl_ref[...] = h[0:1, :]


def rglru_prompt(xr, y, conv_w, conv_b, w_a, w_i, b_a, b_i, lam, *, tc=512):
    G, T, C = xr.shape
    bw = C // N_GATE_BLOCKS
    tc = _tile(T, tc)
    tspec = pl.BlockSpec((None, tc, bw), lambda g, n, t: (g, t, n))
    vspec = pl.BlockSpec((1, bw), lambda g, n, t: (0, n))
    wspec = pl.BlockSpec((None, bw, bw), lambda g, n, t: (n, 0, 0))
    return pl.pallas_call(
        functools.partial(_rglru_prompt_body, tc=tc),
        grid=(G, N_GATE_BLOCKS, T // tc),
        in_specs=[tspec, tspec, pl.BlockSpec((CONV_W, bw), lambda g, n, t: (0, n)), vspec,
                  wspec, wspec, vspec, vspec, vspec],
        out_specs=[tspec, pl.BlockSpec((None, 1, bw), lambda g, n, t: (g, 0, n))],
        out_shape=[jax.ShapeDtypeStruct((G, T, C), BF16), jax.ShapeDtypeStruct((G, 1, C), F32)],
        scratch_shapes=[pltpu.VMEM((tc + SUBLANES, bw), F32), pltpu.VMEM((tc, bw), F32),
                        pltpu.VMEM((tc, bw), F32), pltpu.VMEM((SUBLANES, bw), F32)],
        compiler_params=_cp(("arbitrary", "arbitrary", "arbitrary")),
        name="rglru_prompt",
    )(xr, y, conv_w, conv_b.reshape(1, C), w_a, w_i, b_a.reshape(1, C), b_i.reshape(1, C),
      lam.reshape(1, C))


def _rglru_sample_body(xr_ref, y_ref, s0_ref, s1_ref, s2_ref, h0_ref, cw_ref, cb_ref, wa_ref, wi_ref,
                       ba_ref, bi_ref, lam_ref, hy_ref, h_ref):
    xc = (cb_ref[...] + cw_ref[0:1, :] * s0_ref[...] + cw_ref[1:2, :] * s1_ref[...]
          + cw_ref[2:3, :] * s2_ref[...] + cw_ref[3:4, :] * xr_ref[...])
    a, mult, gi = _rg_gates(xc, wa_ref[...], wi_ref[...], ba_ref[...], bi_ref[...], lam_ref[...])
    h = a * h0_ref[...] + mult * gi * xc
    h_ref[...] = h
    hy_ref[...] = (h * y_ref[...]).astype(hy_ref.dtype)


def rglru_sample(xr, y, state, h0, conv_w, conv_b, w_a, w_i, b_a, b_i, lam):
    _, B, C = xr.shape
    bw = C // N_GATE_BLOCKS
    st = state.reshape(B, (CONV_W - 1) * C)
    tspec = pl.BlockSpec((None, B, bw), lambda n: (0, 0, n))
    sspec = [pl.BlockSpec((B, bw), lambda n, k=k: (0, k * N_GATE_BLOCKS + n)) for k in range(CONV_W - 1)]
    vspec = pl.BlockSpec((1, bw), lambda n: (0, n))
    wspec = pl.BlockSpec((None, bw, bw), lambda n: (n, 0, 0))
    hspec = pl.BlockSpec((B, bw), lambda n: (0, n))
    return pl.pallas_call(
        _rglru_sample_body,
        grid=(N_GATE_BLOCKS,),
        in_specs=[tspec, tspec] + sspec + [hspec, pl.BlockSpec((CONV_W, bw), lambda n: (0, n)), vspec,
                                           wspec, wspec, vspec, vspec, vspec],
        out_specs=[tspec, hspec],
        out_shape=[jax.ShapeDtypeStruct((1, B, C), BF16), jax.ShapeDtypeStruct((B, C), F32)],
        compiler_params=_cp(("arbitrary",)),
        name="rglru_sample",
    )(xr, y, st, st, st, h0, conv_w, conv_b.reshape(1, C), w_a, w_i, b_a.reshape(1, C),
      b_i.reshape(1, C), lam.reshape(1, C))


def _rope128(pe, cos, sin):
    return pe * cos + pltpu.roll(pe, QK_ROPE, 1) * sin


def _kv_body(x_ref, w_ref, g_ref, cos_ref, sin_ref, ckv_ref, kpe_ref, wbf):
    @pl.when((pl.program_id(0) == 0) & (pl.program_id(1) == 0))
    def _():
        wbf[...] = w_ref[...].astype(BF16)

    acc = jnp.dot(x_ref[...], wbf[...], preferred_element_type=F32)
    c = acc[:, :KV_LORA]
    ckv_ref[...] = c * lax.rsqrt(jnp.mean(c * c, axis=-1, keepdims=True) + EPS) * g_ref[...]
    rot = _rope128(acc[:, KV_LORA:], cos_ref[...], sin_ref[...])
    kpe_ref[...] = rot[:, :QK_ROPE]


def kv_latent(hn, w_ext, g_ckv, cos_t, sin_t, *, tm=512):
    G, T, D = hn.shape
    tm = _tile(T, tm)
    NE = w_ext.shape[-1]
    if cos_t.shape[0] == 1:
        tab = pl.BlockSpec((1, LANES), lambda g, i: (0, 0))
    else:
        tab = pl.BlockSpec((tm, LANES), lambda g, i: (i, 0))
    return pl.pallas_call(
        _kv_body, grid=(G, T // tm),
        in_specs=[pl.BlockSpec((None, tm, D), lambda g, i: (g, i, 0)),
                  pl.BlockSpec((D, NE), lambda g, i: (0, 0)),
                  pl.BlockSpec((1, KV_LORA), lambda g, i: (0, 0)), tab, tab],
        out_specs=[pl.BlockSpec((None, tm, KV_LORA), lambda g, i: (g, i, 0)),
                   pl.BlockSpec((None, tm, QK_ROPE), lambda g, i: (g, i, 0))],
        out_shape=[jax.ShapeDtypeStruct((G, T, KV_LORA), F32), jax.ShapeDtypeStruct((G, T, QK_ROPE), F32)],
        scratch_shapes=[pltpu.VMEM((D, NE), BF16)],
        compiler_params=_cp(("arbitrary", "arbitrary")), name="kv_latent",
    )(hn, w_ext, g_ckv.reshape(1, KV_LORA), cos_t, sin_t)


def _kv_heads_body(c_ref, pe_ref, wk_ref, wv_ref, gn_ref, gp_ref, k_ref, v_ref, wk_bf, wv_bf):
    @pl.when((pl.program_id(0) == 0) & (pl.program_id(1) == 0))
    def _():
        wk_bf[...] = wk_ref[...].astype(BF16)
        wv_bf[...] = wv_ref[...].astype(BF16)

    c = c_ref[...].astype(BF16)
    pe = pe_ref[...]
    sspe = jnp.sum(pe * pe, axis=-1, keepdims=True)
    for h in range(N_HEADS):
        kh = jnp.dot(c, wk_bf[:, h * QK_NOPE:(h + 1) * QK_NOPE], preferred_element_type=F32)
        ss = jnp.sum(kh * kh, axis=-1, keepdims=True) + sspe
        inv = lax.rsqrt(ss * (1.0 / QK_HEAD) + EPS)
        k_ref[h, :, 0:QK_NOPE] = (kh * inv * gn_ref[...]).astype(k_ref.dtype)
        k_ref[h, :, QK_NOPE:QK_HEAD] = (pe * inv * gp_ref[...]).astype(k_ref.dtype)
        v_ref[h] = jnp.dot(c, wv_bf[:, h * V_HEAD:(h + 1) * V_HEAD],
                           preferred_element_type=F32).astype(v_ref.dtype)


def kv_heads(ckv, kpe, w_uk, w_uv, g_k, *, tm=512):
    G, T, L = ckv.shape
    tm = _tile(T, tm)
    HN = N_HEADS * QK_NOPE
    HV = N_HEADS * V_HEAD
    return pl.pallas_call(
        _kv_heads_body, grid=(G, T // tm),
        in_specs=[pl.BlockSpec((None, tm, L), lambda g, i: (g, i, 0)),
                  pl.BlockSpec((None, tm, QK_ROPE), lambda g, i: (g, i, 0)),
                  pl.BlockSpec((L, HN), lambda g, i: (0, 0)),
                  pl.BlockSpec((L, HV), lambda g, i: (0, 0)),
                  pl.BlockSpec((1, QK_NOPE), lambda g, i: (0, 0)),
                  pl.BlockSpec((1, QK_ROPE), lambda g, i: (0, 0))],
        out_specs=[pl.BlockSpec((None, N_HEADS, tm, QK_HEAD), lambda g, i: (g, 0, i, 0)),
                   pl.BlockSpec((None, N_HEADS, tm, V_HEAD), lambda g, i: (g, 0, i, 0))],
        out_shape=[jax.ShapeDtypeStruct((G, N_HEADS, T, QK_HEAD), BF16),
                   jax.ShapeDtypeStruct((G, N_HEADS, T, V_HEAD), BF16)],
        scratch_shapes=[pltpu.VMEM((L, HN), BF16), pltpu.VMEM((L, HV), BF16)],
        compiler_params=_cp(("arbitrary", "arbitrary")), name="kv_heads",
    )(ckv, kpe, w_uk.reshape(L, HN), w_uv.reshape(L, HV), g_k[:QK_NOPE].reshape(1, QK_NOPE),
      g_k[QK_NOPE:].reshape(1, QK_ROPE))


def _q_body(cq_ref, w_ref, cos_ref, sin_ref, gn_ref, gp_ref, q_ref, wbf):
    @pl.when((pl.program_id(0) == 0) & (pl.program_id(1) == 0))
    def _():
        wbf[...] = w_ref[...].astype(BF16)

    cq = cq_ref[...]
    cos = cos_ref[...]
    sin = sin_ref[...]
    hw = QK_NOPE + LANES
    for h in range(N_HEADS):
        a = jnp.dot(cq, wbf[:, h * hw:(h + 1) * hw], preferred_element_type=F32)
        nope = a[:, :QK_NOPE]
        rot = _rope128(a[:, QK_NOPE:], cos, sin)
        ss = jnp.sum(nope * nope, axis=-1, keepdims=True) + jnp.sum(rot * rot, axis=-1, keepdims=True)
        inv = lax.rsqrt(ss * (1.0 / QK_HEAD) + EPS) * SCALE
        q_ref[h, :, 0:QK_NOPE] = (nope * inv * gn_ref[...]).astype(q_ref.dtype)
        q_ref[h, :, QK_NOPE:QK_HEAD] = (rot * inv * gp_ref[...])[:, :QK_ROPE].astype(q_ref.dtype)


def q_heads(cq, w_ext, cos_t, sin_t, g_q, out_dtype, *, tm=512):
    G, T, L = cq.shape
    tm = _tile(T, tm)
    NE = w_ext.shape[-1]
    if cos_t.shape[0] == 1:
        tab = pl.BlockSpec((1, LANES), lambda g, i: (0, 0))
    else:
        tab = pl.BlockSpec((tm, LANES), lambda g, i: (i, 0))
    gp = jnp.zeros((1, LANES), F32).at[0, :QK_ROPE].set(g_q[QK_NOPE:])
    return pl.pallas_call(
        _q_body, grid=(G, T // tm),
        in_specs=[pl.BlockSpec((None, tm, L), lambda g, i: (g, i, 0)),
                  pl.BlockSpec((L, NE), lambda g, i: (0, 0)), tab, tab,
                  pl.BlockSpec((1, QK_NOPE), lambda g, i: (0, 0)),
                  pl.BlockSpec((1, LANES), lambda g, i: (0, 0))],
        out_specs=pl.BlockSpec((None, N_HEADS, tm, QK_HEAD), lambda g, i: (g, 0, i, 0)),
        out_shape=jax.ShapeDtypeStruct((G, N_HEADS, T, QK_HEAD), out_dtype),
        scratch_shapes=[pltpu.VMEM((L, NE), BF16)],
        compiler_params=_cp(("arbitrary", "arbitrary")), name="q_heads",
    )(cq, w_ext, cos_t, sin_t, g_q[:QK_NOPE].reshape(1, QK_NOPE), gp)


def _flash_body(q_ref, k_ref, v_ref, o_ref, *, tq):
    i = pl.program_id(2)
    q = q_ref[...]

    def step(j, carry, masked):
        m, l, acc = carry
        r0 = pl.multiple_of(j * tq, tq)
        k = k_ref[pl.ds(r0, tq), :]
        v = v_ref[pl.ds(r0, tq), :]
        s = lax.dot_general(q, k, NT_DIMS, preferred_element_type=F32)
        if masked:
            row = lax.broadcasted_iota(jnp.int32, (tq, tq), 0)
            col = lax.broadcasted_iota(jnp.int32, (tq, tq), 1)
            s = jnp.where(col <= row, s, -jnp.inf)
        m_new = jnp.maximum(m, jnp.max(s, axis=-1, keepdims=True))
        p = jnp.exp(s - m_new)
        corr = jnp.exp(m - m_new)
        l = l * corr + jnp.sum(p, axis=-1, keepdims=True)
        acc = acc * corr + jnp.dot(p.astype(BF16), v, preferred_element_type=F32)
        return m_new, l, acc

    init = (jnp.full((tq, 1), -jnp.inf, F32), jnp.zeros((tq, 1), F32), jnp.zeros((tq, V_HEAD), F32))
    carry = lax.fori_loop(0, i, lambda j, c: step(j, c, False), init)
    m, l, acc = step(i, carry, True)
    o_ref[...] = (acc / l).astype(o_ref.dtype)


def flash_prompt(q, k, v, *, tq=512):
    G, H, T, E = q.shape
    tq = _tile(T, tq)
    return pl.pallas_call(
        functools.partial(_flash_body, tq=tq),
        grid=(G, H, T // tq),
        in_specs=[pl.BlockSpec((None, None, tq, E), lambda g, h, i: (g, h, i, 0)),
                  pl.BlockSpec((None, None, T, E), lambda g, h, i: (g, h, 0, 0)),
                  pl.BlockSpec((None, None, T, V_HEAD), lambda g, h, i: (g, h, 0, 0))],
        out_specs=pl.BlockSpec((None, tq, V_HEAD), lambda g, h, i: (g, i, h)),
        out_shape=jax.ShapeDtypeStruct((G, T, H * V_HEAD), BF16),
        compiler_params=_cp(("arbitrary", "arbitrary", "arbitrary")), name="flash_prompt",
    )(q, k, v)


def _absorb_body(q_ref, k_ref, w_ref, gn_ref, gp_ref, qlat_ref, qpe_ref, sn_ref):
    q = q_ref[...]
    qn = (q[:, :QK_NOPE] * gn_ref[...]).astype(BF16)
    qlat_ref[...] = lax.dot_general(qn, w_ref[...].astype(BF16), NT_DIMS,
                                    preferred_element_type=F32).astype(qlat_ref.dtype)
    qpe_ref[...] = (q[:, QK_NOPE:] * gp_ref[...]).astype(qpe_ref.dtype)
    s = jnp.sum(q.astype(BF16).astype(F32) * k_ref[...].astype(F32), axis=-1, keepdims=True)
    sn_ref[...] = jnp.broadcast_to(s, sn_ref.shape)


def absorb_queries(q, k, w_uk, g_k):
    _, H, B, E = q.shape
    L = w_uk.shape[0]
    return pl.pallas_call(
        _absorb_body, grid=(H,),
        in_specs=[pl.BlockSpec((None, None, B, E), lambda h: (0, h, 0, 0)),
                  pl.BlockSpec((None, None, B, E), lambda h: (0, h, 0, 0)),
                  pl.BlockSpec((L, QK_NOPE), lambda h: (0, h)),
                  pl.BlockSpec((1, QK_NOPE), lambda h: (0, 0)),
                  pl.BlockSpec((1, QK_ROPE), lambda h: (0, 0))],
        out_specs=[pl.BlockSpec((None, B, L), lambda h: (h, 0, 0)),
                   pl.BlockSpec((None, B, QK_ROPE), lambda h: (h, 0, 0)),
                   pl.BlockSpec((None, B, LANES), lambda h: (h, 0, 0))],
        out_shape=[jax.ShapeDtypeStruct((H, B, L), BF16), jax.ShapeDtypeStruct((H, B, QK_ROPE), BF16),
                   jax.ShapeDtypeStruct((H, B, LANES), F32)],
        compiler_params=_cp(("arbitrary",)), name="absorb_queries",
    )(q, k, w_uk.reshape(L, H * QK_NOPE), g_k[:QK_NOPE].reshape(1, QK_NOPE),
      g_k[QK_NOPE:].reshape(1, QK_ROPE))


def _paged_body(pt_ref, *refs, pp):
    ck_refs = refs[:pp]
    kp_refs = refs[pp:2 * pp]
    (qlat_ref, qpe_ref, sn_ref, cnew_ref, wt_ref, o_ref, waug, m_s, l_s, acc_s) = refs[2 * pp:]
    n = pl.program_id(0)
    p = pl.program_id(1)
    HN = N_HEADS * QK_NOPE

    @pl.when((n == 0) & (p == 0))
    def _():
        waug[0:HN, :] = wt_ref[...]

    @pl.when(p == 0)
    def _():
        waug[HN:HN + N_HEADS, :] = qlat_ref[...]
        m_s[...] = sn_ref[...]
        l_s[...] = jnp.ones(l_s.shape, F32)
        acc_s[...] = jnp.broadcast_to(cnew_ref[...], acc_s.shape)

    cc = jnp.concatenate([r[...].astype(BF16) for r in ck_refs], axis=0)
    kpt = jnp.concatenate([r[...] for r in kp_refs], axis=1)
    P = cc.shape[0]
    kt = lax.dot_general(waug[...], cc, NT_DIMS, preferred_element_type=F32)
    kn = kt[0:HN, :].reshape(N_HEADS, QK_NOPE, P)
    ssn = jnp.sum(kn * kn, axis=1)
    s_lat = kt[HN:HN + N_HEADS, :]
    sspe = jnp.sum(kpt * kpt, axis=0, keepdims=True)
    s_pe = jnp.dot(qpe_ref[...], kpt.astype(BF16), preferred_element_type=F32)
    inv = lax.rsqrt((ssn + sspe) * (1.0 / QK_HEAD) + EPS)
    s = (s_lat + s_pe) * inv
    m_old = m_s[...]
    m_new = jnp.maximum(m_old, jnp.max(s, axis=-1, keepdims=True))
    pexp = jnp.exp(s - m_new[:, 0:1])
    corr = jnp.exp(m_old - m_new)
    l_s[...] = l_s[...] * corr + jnp.sum(pexp, axis=-1, keepdims=True)
    acc_s[...] = acc_s[...] * corr[:, 0:1] + jnp.dot(pexp.astype(BF16), cc, preferred_element_type=F32)
    m_s[...] = m_new

    @pl.when(p == pl.num_programs(1) - 1)
    def _():
        o_ref[...] = acc_s[...] / l_s[:, 0:1]


def paged_attention(page_table, cache_ckv, cache_kpe_t, qlat, qpe, sn, c_new, w_uk_t, *, pp=16):
    B, NP = page_table.shape
    L = cache_ckv.shape[-1]
    pp = pp if NP % pp == 0 else 2
    HN = N_HEADS * QK_NOPE
    ck_specs = [pl.BlockSpec((None, PAGE, L), lambda n, p, pt, k=k: (pt[n, p * pp + k], 0, 0))
                for k in range(pp)]
    kp_specs = [pl.BlockSpec((None, QK_ROPE, PAGE), lambda n, p, pt, k=k: (pt[n, p * pp + k], 0, 0))
                for k in range(pp)]
    per_seq = lambda w: pl.BlockSpec((None, N_HEADS, w), lambda n, p, pt: (n, 0, 0))
    grid_spec = pltpu.PrefetchScalarGridSpec(
        num_scalar_prefetch=1,
        grid=(B, NP // pp),
        in_specs=ck_specs + kp_specs + [
            per_seq(L), per_seq(QK_ROPE), per_seq(LANES),
            pl.BlockSpec((None, 1, L), lambda n, p, pt: (n, 0, 0)),
            pl.BlockSpec((HN, L), lambda n, p, pt: (0, 0))],
        out_specs=per_seq(L),
        scratch_shapes=[pltpu.VMEM((HN + N_HEADS, L), BF16), pltpu.VMEM((N_HEADS, LANES), F32),
                        pltpu.VMEM((N_HEADS, LANES), F32), pltpu.VMEM((N_HEADS, L), F32)],
    )
    return pl.pallas_call(
        functools.partial(_paged_body, pp=pp),
        grid_spec=grid_spec,
        out_shape=jax.ShapeDtypeStruct((B, N_HEADS, L), F32),
        compiler_params=_cp(("arbitrary", "arbitrary")), name="paged_attention",
    )(page_table, *([cache_ckv] * pp), *([cache_kpe_t] * pp), qlat, qpe, sn, c_new, w_uk_t)


def _head_proj_body(x_ref, w_ref, o_ref):
    o_ref[...] = jnp.dot(x_ref[...].astype(BF16), w_ref[...].astype(BF16),
                         preferred_element_type=F32).astype(o_ref.dtype)


def head_value_proj(o_lat, w_uv):
    H, B, L = o_lat.shape
    return pl.pallas_call(
        _head_proj_body, grid=(H,),
        in_specs=[pl.BlockSpec((None, B, L), lambda h: (h, 0, 0)),
                  pl.BlockSpec((L, V_HEAD), lambda h: (0, h))],
        out_specs=pl.BlockSpec((None, B, V_HEAD), lambda h: (0, 0, h)),
        out_shape=jax.ShapeDtypeStruct((1, B, H * V_HEAD), BF16),
        compiler_params=_cp(("arbitrary",)), name="head_value_proj",
    )(o_lat, w_uv)


def _new_expert(te_ref, r):
    prev = te_ref[jnp.maximum(r - 1, 0)]
    return (r == 0) | (te_ref[r] != prev)


def _moe_up_body(te_ref, tv_ref, x_ref, wg_ref, wu_ref, o_ref, wg_bf, wu_bf):
    r = pl.program_id(1)

    @pl.when(_new_expert(te_ref, r))
    def _():
        wg_bf[...] = wg_ref[...].astype(BF16)
        wu_bf[...] = wu_ref[...].astype(BF16)

    @pl.when(tv_ref[r] > 0)
    def _():
        x = x_ref[...]
        a = jnp.dot(x, wg_bf[...], preferred_element_type=F32)
        u = jnp.dot(x, wu_bf[...], preferred_element_type=F32)
        o_ref[...] = (a * jax.nn.sigmoid(a) * u).astype(o_ref.dtype)

    @pl.when(tv_ref[r] == 0)
    def _():
        o_ref[...] = jnp.zeros(o_ref.shape, o_ref.dtype)


def _moe_down_body(te_ref, tv_ref, h_ref, wd_ref, o_ref, wd_bf):
    r = pl.program_id(1)

    @pl.when(_new_expert(te_ref, r))
    def _():
        wd_bf[...] = wd_ref[...].astype(BF16)

    @pl.when(tv_ref[r] > 0)
    def _():
        o_ref[...] = jnp.dot(h_ref[...], wd_bf[...], preferred_element_type=F32)

    @pl.when(tv_ref[r] == 0)
    def _():
        o_ref[...] = jnp.zeros(o_ref.shape, o_ref.dtype)


def moe_experts(xg, tile_expert, tile_valid, w_gate, w_up, w_down, *, tm, tf=512, tn=512):
    R, D = xg.shape
    F = w_gate.shape[-1]
    tf = _tile(F, tf)
    tn = _tile(D, tn)
    up_spec = pltpu.PrefetchScalarGridSpec(
        num_scalar_prefetch=2, grid=(F // tf, R // tm),
        in_specs=[pl.BlockSpec((tm, D), lambda f, r, te, tv: (r, 0)),
                  pl.BlockSpec((None, D, tf), lambda f, r, te, tv: (te[r], 0, f)),
                  pl.BlockSpec((None, D, tf), lambda f, r, te, tv: (te[r], 0, f))],
        out_specs=pl.BlockSpec((tm, tf), lambda f, r, te, tv: (r, f)),
        scratch_shapes=[pltpu.VMEM((D, tf), BF16), pltpu.VMEM((D, tf), BF16)])
    h = pl.pallas_call(
        _moe_up_body, grid_spec=up_spec, out_shape=jax.ShapeDtypeStruct((R, F), BF16),
        compiler_params=_cp(("arbitrary", "arbitrary")), name="moe_up",
    )(tile_expert, tile_valid, xg, w_gate, w_up)
    down_spec = pltpu.PrefetchScalarGridSpec(
        num_scalar_prefetch=2, grid=(D // tn, R // tm),
        in_specs=[pl.BlockSpec((tm, F), lambda j, r, te, tv: (r, 0)),
                  pl.BlockSpec((None, F, tn), lambda j, r, te, tv: (te[r], 0, j))],
        out_specs=pl.BlockSpec((tm, tn), lambda j, r, te, tv: (r, j)),
        scratch_shapes=[pltpu.VMEM((F, tn), BF16)])
    return pl.pallas_call(
        _moe_down_body, grid_spec=down_spec, out_shape=jax.ShapeDtypeStruct((R, D), F32),
        compiler_params=_cp(("arbitrary", "arbitrary")), name="moe_down",
    )(tile_expert, tile_valid, h, w_down)


def _combine_body(x_ref, g_ref, ya_ref, yb_ref, w_ref, o_ref):
    w = w_ref[...]
    f = w[:, 0:1] * ya_ref[...] + w[:, 1:2] * yb_ref[...]
    o_ref[...] = x_ref[...] + g_ref[...] * f


def moe_combine(x, gate_mod, gate_chunk, ya, yb, w, *, tm=256):
    G, T, D = x.shape
    R = gate_mod.shape[1]
    tm = _tile(T, tm)
    tspec = pl.BlockSpec((None, tm, D), lambda g, i: (g, i, 0))
    return pl.pallas_call(
        _combine_body, grid=(G, T // tm),
        in_specs=[tspec, pl.BlockSpec((None, R, D), lambda g, i: (g, 0, gate_chunk)), tspec, tspec,
                  pl.BlockSpec((None, tm, LANES), lambda g, i: (g, i, 0))],
        out_specs=tspec, out_shape=jax.ShapeDtypeStruct((G, T, D), F32),
        compiler_params=_cp(("arbitrary", "arbitrary")), name="moe_combine",
    )(x, gate_mod, ya, yb, w)


def _moe_layout(ids, tm):
    A = ids.shape[0]
    R = -(-(A + N_EXPERTS * (tm - 1)) // tm) * tm
    onehot = (ids[:, None] == jnp.arange(N_EXPERTS, dtype=jnp.int32)[None, :]).astype(jnp.int32)
    rank = jnp.cumsum(onehot, axis=0) - onehot
    counts = jnp.sum(onehot, axis=0)
    tiles = (counts + tm - 1) // tm
    tile_end = jnp.cumsum(tiles)
    row_off = (tile_end - tiles) * tm
    pos = jnp.sum(onehot * (row_off[None, :] + rank), axis=1)
    tile_id = jnp.arange(R // tm, dtype=jnp.int32)
    tile_expert = jnp.minimum(jnp.sum((tile_id[:, None] >= tile_end[None, :]).astype(jnp.int32), axis=1),
                              N_EXPERTS - 1).astype(jnp.int32)
    tile_valid = (tile_id < tile_end[-1]).astype(jnp.int32)
    return R, pos.astype(jnp.int32), tile_expert, tile_valid


def _rope_tables(pos):
    half = QK_ROPE // 2
    inv = ROPE_THETA ** (-jnp.arange(half, dtype=F32) / half)
    ang = pos.astype(F32)[:, None] * inv[None, :]
    cos, sin = jnp.cos(ang), jnp.sin(ang)
    z = jnp.zeros((pos.shape[0], LANES - QK_ROPE), F32)
    return jnp.concatenate([cos, cos, z], axis=-1), jnp.concatenate([-sin, sin, z], axis=-1)


def _swap_halves(w):
    half = QK_ROPE // 2
    return jnp.concatenate([w[..., half:], w[..., :half]], axis=-1)


def kernel(x_prompt, x_sample, state_rglru_h, state_rglru_conv, cache_ckv, cache_kpe, page_table, c_prompt, c_sample, w_ada, b_ada, g_norm, rg_w_y, rg_b_y, rg_w_x, rg_b_x, rg_conv_w, rg_conv_b, rg_w_a, rg_b_a, rg_w_i, rg_b_i, rg_lam, rg_w_out, rg_b_out, kv_w_ada, kv_b_ada, kv_g_in, kv_w_dkv, kv_g_ckv, kv_w_uk, kv_w_uv, kv_g_k, mla_w_dq, mla_g_cq, mla_w_uq, mla_g_q, mla_w_o, ffn_w_gate, ffn_w_up, ffn_w_down, moe_w_router, moe_w_gate, moe_w_up, moe_w_down):
    Bp, S, D = x_prompt.shape
    Bs = x_sample.shape[0]
    past = page_table.shape[1] * PAGE
    HN = N_HEADS * QK_NOPE
    HV = N_HEADS * V_HEAD

    xs = {"p": x_prompt, "s": x_sample.reshape(1, Bs, D)}
    groups = ("p", "s")

    pad = (-(Bs + Bp)) % 16
    c_all = jnp.concatenate([c_sample, c_prompt, jnp.zeros((pad, D), F32)], axis=0)[None]

    def split_mods(m):
        return {"p": m[0, Bs:Bs + Bp].reshape(Bp, 1, -1), "s": m[:, :Bs]}

    mods = [split_mods(mm(c_all, w_ada, (l,), bias=b_ada[l], pre_silu=True, name="adaln"))
            for l in range(2)]
    mods_kv = split_mods(mm(c_all, kv_w_ada, bias=kv_b_ada, pre_silu=True, name="adaln_kv"))

    x1 = {}
    h_last = {}
    conv_last = {}
    for gname in groups:
        x = xs[gname]
        md = mods[0][gname]
        xn = norm_mod(x, g_norm[0, 0], md, 0, 1)
        y = mm(xn, rg_w_y, (0,), bias=rg_b_y[0], act="gelu", name="rg_y")
        xr = mm(xn, rg_w_x, (0,), bias=rg_b_x[0], name="rg_x")
        if gname == "p":
            hy, hl = rglru_prompt(xr, y, rg_conv_w[0], rg_conv_b[0], rg_w_a[0], rg_w_i[0], rg_b_a[0],
                                  rg_b_i[0], rg_lam[0])
            h_last[gname] = hl.reshape(1, Bp, D)
            conv_last[gname] = xr[:, S - (CONV_W - 1):, :][None]
        else:
            hy, hl = rglru_sample(xr, y, state_rglru_conv[0], state_rglru_h[0], rg_conv_w[0], rg_conv_b[0],
                                  rg_w_a[0], rg_w_i[0], rg_b_a[0], rg_b_i[0], rg_lam[0])
            h_last[gname] = hl[None]
            conv_last[gname] = jnp.concatenate([state_rglru_conv[0][:, 1:], xr.reshape(Bs, 1, D)], axis=1)[None]
        xa = mm(hy, rg_w_out, (0,), bias=rg_b_out[0], res=x, gate=md, gate_chunk=2, name="rg_out")
        xn = norm_mod(xa, g_norm[0, 1], md, 3, 4)
        hmid = swiglu_up(xn, ffn_w_gate, ffn_w_up, (0,))
        x1[gname] = mm(hmid, ffn_w_down, (0,), res=xa, gate=md, gate_chunk=5, name="ffn_down")

    w_dkv_ext = jnp.concatenate([kv_w_dkv, _swap_halves(kv_w_dkv[:, KV_LORA:])], axis=-1)
    w_uq = mla_w_uq[0]
    w_uq_ext = jnp.concatenate([w_uq, _swap_halves(w_uq[..., QK_NOPE:])], axis=-1).reshape(
        w_uq.shape[0], N_HEADS * (QK_NOPE + LANES))
    tabs = {"p": _rope_tables(jnp.arange(S)), "s": _rope_tables(jnp.full((1,), past))}
    w_uk2 = kv_w_uk.reshape(KV_LORA, HN)
    w_uv2 = kv_w_uv.reshape(KV_LORA, HV)

    ckv = {}
    kpe = {}
    xb = {}
    for gname in groups:
        x = x1[gname]
        md = mods[1][gname]
        cos_t, sin_t = tabs[gname]
        hn = norm_mod(x, kv_g_in, mods_kv[gname], 0, 1)
        ckv[gname], kpe[gname] = kv_latent(hn, w_dkv_ext, kv_g_ckv, cos_t, sin_t)
        k, v = kv_heads(ckv[gname], kpe[gname], w_uk2, w_uv2, kv_g_k)
        xn = norm_mod(x, g_norm[1, 0], md, 0, 1)
        cq = mm(xn, mla_w_dq, (0,), rms_g=mla_g_cq[0], out_dtype=BF16, name="q_down")
        if gname == "p":
            q = q_heads(cq, w_uq_ext, cos_t, sin_t, mla_g_q[0], BF16)
            o = flash_prompt(q, k, v)
        else:
            q = q_heads(cq, w_uq_ext, cos_t, sin_t, mla_g_q[0], F32)
            qlat, qpe, sn = absorb_queries(q, k, w_uk2, kv_g_k)
            o_lat = paged_attention(page_table, cache_ckv, jnp.swapaxes(cache_kpe, 1, 2), jnp.swapaxes(qlat, 0, 1),
                                    jnp.swapaxes(qpe, 0, 1), jnp.swapaxes(sn, 0, 1),
                                    ckv[gname].reshape(Bs, 1, KV_LORA), w_uk2.T.astype(BF16))
            o = head_value_proj(jnp.swapaxes(o_lat, 0, 1), w_uv2)
        xb[gname] = mm(o, mla_w_o, (0,), res=x, gate=md, gate_chunk=2, name="attn_out")

    xn_l, gate_l, idx_l = [], [], []
    for gname in groups:
        xn, gt, ix = norm_mod(xb[gname], g_norm[1, 1], mods[1][gname], 3, 4, router_w=moe_w_router[0])
        xn_l.append(xn.reshape(-1, D))
        gate_l.append(gt)
        idx_l.append(ix[..., :2].reshape(-1, 2))
    xn_all = jnp.concatenate(xn_l, axis=0)
    ids = jnp.concatenate(idx_l, axis=0).reshape(-1)
    tm_moe = 512
    R, pos, tile_expert, tile_valid = _moe_layout(ids, tm_moe)
    tok_of_row = jnp.zeros((R,), jnp.int32).at[pos].set(jnp.arange(ids.shape[0], dtype=jnp.int32) // 2)
    xg = xn_all.at[tok_of_row].get(mode="promise_in_bounds")
    yg = moe_experts(xg, tile_expert, tile_valid, moe_w_gate[0], moe_w_up[0], moe_w_down[0], tm=tm_moe)
    pos2 = pos.reshape(-1, 2)
    n_p = Bp * S
    out = {}
    for gi, gname in enumerate(groups):
        sl = slice(0, n_p) if gname == "p" else slice(n_p, n_p + Bs)
        shp = xb[gname].shape
        ya = yg.at[pos2[sl, 0]].get(mode="promise_in_bounds").reshape(shp)
        yb = yg.at[pos2[sl, 1]].get(mode="promise_in_bounds").reshape(shp)
        out[gname] = moe_combine(xb[gname], mods[1][gname], 5, ya, yb, gate_l[gi])

    return (out["p"], out["s"].reshape(Bs, 1, D),
            h_last["p"], conv_last["p"], ckv["p"], kpe["p"],
            h_last["s"], conv_last["s"], ckv["s"].reshape(Bs, 1, KV_LORA), kpe["s"].reshape(Bs, 1, QK_ROPE))
```

```python
import functools
import math

import jax
import jax.numpy as jnp
from jax import lax
from jax.experimental import pallas as pl
from jax.experimental.pallas import tpu as pltpu

F32 = jnp.float32
BF16 = jnp.bfloat16

N_GATE_BLOCKS = 8
CONV_W = 4
RG_C = 8.0
N_HEADS = 16
KV_LORA = 512
QK_NOPE = 128
QK_ROPE = 64
QK_HEAD = QK_NOPE + QK_ROPE
V_HEAD = 128
ROPE_THETA = 10000.0
SCALE = QK_HEAD ** -0.5
N_EXPERTS = 8
EPS = 1e-6
PAGE = 128
LANES = 128
SUBLANES = 8
VMEM_LIMIT = 56 * 1024 * 1024
NT_DIMS = (((1,), (1,)), ((), ()))


def _cp(sem):
    return pltpu.CompilerParams(dimension_semantics=sem, vmem_limit_bytes=VMEM_LIMIT)


def _tile(n, pref):
    return pref if n % pref == 0 else n


def _mm_body(*refs, act, has_bias, has_res, has_rms, pre_silu):
    x_ref, w_ref = refs[0], refs[1]
    k = 2
    b_ref = res_ref = gate_ref = g_ref = None
    if has_bias:
        b_ref = refs[k]; k += 1
    if has_res:
        res_ref, gate_ref = refs[k], refs[k + 1]; k += 2
    if has_rms:
        g_ref = refs[k]; k += 1
    o_ref, wbf = refs[k], refs[k + 1]

    @pl.when((pl.program_id(1) == 0) & (pl.program_id(2) == 0))
    def _():
        wbf[...] = w_ref[...].astype(BF16)

    x = x_ref[...]
    if pre_silu:
        x = x * jax.nn.sigmoid(x)
    acc = jnp.dot(x.astype(BF16), wbf[...], preferred_element_type=F32)
    if has_bias:
        acc = acc + b_ref[...]
    if act == "gelu":
        acc = jax.nn.gelu(acc)
    if has_rms:
        acc = acc * lax.rsqrt(jnp.mean(acc * acc, axis=-1, keepdims=True) + EPS) * g_ref[...]
    if has_res:
        acc = res_ref[...] + gate_ref[...] * acc
    o_ref[...] = acc.astype(o_ref.dtype)


def mm(x, w, widx=(), *, bias=None, act=None, res=None, gate=None, gate_chunk=0,
       rms_g=None, pre_silu=False, out_dtype=F32, tm=512, tn=512, name="mm"):
    G, T, K = x.shape
    N = w.shape[-1]
    tm = _tile(T, tm)
    tn = _tile(N, tn)
    nb = N // tn
    widx = tuple(widx)
    in_specs = [
        pl.BlockSpec((None, tm, K), lambda j, g, i: (g, i, 0)),
        pl.BlockSpec((None,) * len(widx) + (K, tn), lambda j, g, i: widx + (0, j)),
    ]
    args = [x, w]
    if bias is not None:
        in_specs.append(pl.BlockSpec((1, tn), lambda j, g, i: (0, j)))
        args.append(bias.reshape(1, N))
    if res is not None:
        R = gate.shape[1]
        in_specs.append(pl.BlockSpec((None, tm, tn), lambda j, g, i: (g, i, j)))
        in_specs.append(pl.BlockSpec((None, R, tn), lambda j, g, i: (g, 0, gate_chunk * nb + j)))
        args += [res, gate]
    if rms_g is not None:
        assert tn == N
        in_specs.append(pl.BlockSpec((1, tn), lambda j, g, i: (0, j)))
        args.append(rms_g.reshape(1, N))
    body = functools.partial(_mm_body, act=act, has_bias=bias is not None, has_res=res is not None,
                             has_rms=rms_g is not None, pre_silu=pre_silu)
    return pl.pallas_call(
        body,
        grid=(nb, G, T // tm),
        in_specs=in_specs,
        out_specs=pl.BlockSpec((None, tm, tn), lambda j, g, i: (g, i, j)),
        out_shape=jax.ShapeDtypeStruct((G, T, N), out_dtype),
        scratch_shapes=[pltpu.VMEM((K, tn), BF16)],
        compiler_params=_cp(("arbitrary", "arbitrary", "arbitrary")),
        name=name,
    )(*args)


def _swiglu_body(x_ref, wg_ref, wu_ref, o_ref, wg_bf, wu_bf):
    @pl.when((pl.program_id(1) == 0) & (pl.program_id(2) == 0))
    def _():
        wg_bf[...] = wg_ref[...].astype(BF16)
        wu_bf[...] = wu_ref[...].astype(BF16)

    x = x_ref[...]
    a = jnp.dot(x, wg_bf[...], preferred_element_type=F32)
    u = jnp.dot(x, wu_bf[...], preferred_element_type=F32)
    o_ref[...] = (a * jax.nn.sigmoid(a) * u).astype(o_ref.dtype)


def swiglu_up(x, wg, wu, widx, *, tm=512, tn=512):
    G, T, K = x.shape
    N = wg.shape[-1]
    tm = _tile(T, tm)
    tn = _tile(N, tn)
    widx = tuple(widx)
    wspec = pl.BlockSpec((None,) * len(widx) + (K, tn), lambda j, g, i: widx + (0, j))
    return pl.pallas_call(
        _swiglu_body,
        grid=(N // tn, G, T // tm),
        in_specs=[pl.BlockSpec((None, tm, K), lambda j, g, i: (g, i, 0)), wspec, wspec],
        out_specs=pl.BlockSpec((None, tm, tn), lambda j, g, i: (g, i, j)),
        out_shape=jax.ShapeDtypeStruct((G, T, N), BF16),
        scratch_shapes=[pltpu.VMEM((K, tn), BF16), pltpu.VMEM((K, tn), BF16)],
        compiler_params=_cp(("arbitrary", "arbitrary", "arbitrary")),
        name="swiglu_up",
    )(x, wg, wu)


def _modulated_norm(x, g, shift, scale):
    y = x * lax.rsqrt(jnp.mean(x * x, axis=-1, keepdims=True) + EPS)
    return (y * g) * (1.0 + scale) + shift


def _norm_body(x_ref, g_ref, sh_ref, sc_ref, o_ref):
    o_ref[...] = _modulated_norm(x_ref[...], g_ref[...], sh_ref[...], sc_ref[...]).astype(o_ref.dtype)


def _norm_router_body(x_ref, g_ref, sh_ref, sc_ref, wr_ref, o_ref, gate_ref, idx_ref):
    xn = _modulated_norm(x_ref[...], g_ref[...], sh_ref[...], sc_ref[...])
    o_ref[...] = xn.astype(o_ref.dtype)
    logits = jnp.dot(xn, wr_ref[...], preferred_element_type=F32, precision=lax.Precision.HIGHEST)
    lane_i = lax.broadcasted_iota(jnp.int32, logits.shape, 1)
    lane = lane_i.astype(F32)
    neg = jnp.float32(-jnp.inf)
    lg = jnp.where(lane_i < N_EXPERTS, logits, neg)
    m1 = jnp.max(lg, axis=-1, keepdims=True)
    i1 = jnp.min(jnp.where(lg == m1, lane, float(LANES)), axis=-1, keepdims=True)
    lg2 = jnp.where(lane == i1, neg, lg)
    m2 = jnp.max(lg2, axis=-1, keepdims=True)
    i2 = jnp.min(jnp.where(lg2 == m2, lane, float(LANES)), axis=-1, keepdims=True)
    e2 = jnp.exp(m2 - m1)
    g1 = 1.0 / (1.0 + e2)
    g2 = e2 * g1
    gate_ref[...] = jnp.where(lane_i == 0, g1, jnp.where(lane_i == 1, g2, 0.0))
    idx_ref[...] = jnp.where(lane_i == 0, i1, jnp.where(lane_i == 1, i2, 0.0)).astype(jnp.int32)


def norm_mod(x, gain, mods, k_shift, k_scale, *, router_w=None, tm=256):
    G, T, D = x.shape
    R = mods.shape[1]
    tm = _tile(T, tm)
    if R != 1:
        assert R == T and tm == T
    in_specs = [
        pl.BlockSpec((None, tm, D), lambda g, i: (g, i, 0)),
        pl.BlockSpec((1, D), lambda g, i: (0, 0)),
        pl.BlockSpec((None, R, D), lambda g, i: (g, 0, k_shift)),
        pl.BlockSpec((None, R, D), lambda g, i: (g, 0, k_scale)),
    ]
    args = [x, gain.reshape(1, D), mods, mods]
    xspec = pl.BlockSpec((None, tm, D), lambda g, i: (g, i, 0))
    if router_w is None:
        return pl.pallas_call(
            _norm_body, grid=(G, T // tm), in_specs=in_specs, out_specs=xspec,
            out_shape=jax.ShapeDtypeStruct((G, T, D), BF16),
            compiler_params=_cp(("arbitrary", "arbitrary")), name="norm_mod",
        )(*args)
    wr = jnp.zeros((D, LANES), F32).at[:, :N_EXPERTS].set(router_w)
    in_specs.append(pl.BlockSpec((D, LANES), lambda g, i: (0, 0)))
    lspec = pl.BlockSpec((None, tm, LANES), lambda g, i: (g, i, 0))
    return pl.pallas_call(
        _norm_router_body, grid=(G, T // tm), in_specs=in_specs,
        out_specs=[xspec, lspec, lspec],
        out_shape=[jax.ShapeDtypeStruct((G, T, D), BF16),
                   jax.ShapeDtypeStruct((G, T, LANES), F32),
                   jax.ShapeDtypeStruct((G, T, LANES), jnp.int32)],
        compiler_params=_cp(("arbitrary", "arbitrary")), name="norm_router",
    )(*args, wr)


def _rg_gates(xc, wa, wi, ba, bi, lam):
    xb = xc.astype(BF16)
    r = jax.nn.sigmoid(jnp.dot(xb, wa.astype(BF16), preferred_element_type=F32) + ba)
    gi = jax.nn.sigmoid(jnp.dot(xb, wi.astype(BF16), preferred_element_type=F32) + bi)
    neg_lam = -lam
    softplus = jnp.maximum(neg_lam, 0.0) + jnp.log1p(jnp.exp(-jnp.abs(neg_lam)))
    log_a = -RG_C * r * softplus
    a = jnp.exp(log_a)
    th = jnp.tanh(log_a)
    mult = jnp.sqrt(-2.0 * th / (1.0 - th))
    return a, mult, gi


def _rglru_prompt_body(xr_ref, y_ref, cw_ref, cb_ref, wa_ref, wi_ref, ba_ref, bi_ref, lam_ref,
                       hy_ref, hl_ref, ext, a_s, b_s, h_s, *, tc):
    t = pl.program_id(2)
    C = xr_ref.shape[-1]

    @pl.when(t == 0)
    def _():
        ext[0:SUBLANES, :] = jnp.zeros((SUBLANES, C), F32)
        h_s[...] = jnp.zeros((SUBLANES, C), F32)

    ext[SUBLANES:SUBLANES + tc, :] = xr_ref[...]
    xc = cb_ref[...]
    for k in range(CONV_W):
        off = SUBLANES - (CONV_W - 1) + k
        xc = xc + cw_ref[k:k + 1, :] * ext[off:off + tc, :]
    ext[0:SUBLANES, :] = ext[tc:tc + SUBLANES, :]

    a, mult, gi = _rg_gates(xc, wa_ref[...], wi_ref[...], ba_ref[...], bi_ref[...], lam_ref[...])
    pos = t * tc + lax.broadcasted_iota(jnp.int32, (tc, 1), 0)
    mult = jnp.where(pos == 0, 1.0, mult)
    a_s[...] = a
    b_s[...] = mult * gi * xc

    ridx = lax.broadcasted_iota(jnp.int32, (SUBLANES, C), 0)

    def group(gidx, h):
        r0 = pl.multiple_of(gidx * SUBLANES, SUBLANES)
        a8 = a_s[pl.ds(r0, SUBLANES), :]
        b8 = b_s[pl.ds(r0, SUBLANES), :]
        for s in (1, 2, 4):
            keep = ridx >= s
            a_sh = jnp.where(keep, pltpu.roll(a8, s, 0), 1.0)
            b_sh = jnp.where(keep, pltpu.roll(b8, s, 0), 0.0)
            b8 = a8 * b_sh + b8
            a8 = a8 * a_sh
        h8 = a8 * h + b8
        hy_ref[pl.ds(r0, SUBLANES), :] = (h8 * y_ref[pl.ds(r0, SUBLANES), :]).astype(hy_ref.dtype)
        return jnp.broadcast_to(h8[SUBLANES - 1:SUBLANES, :], h8.shape)

    h = lax.fori_loop(0, tc // SUBLANES, group, h_s[...], unroll=4)
    h_s[...] = h
    hl_ref[...] = h[0:1, :]


def rglru_prompt(xr, y, conv_w, conv_b, w_a, w_i, b_a, b_i, lam, *, tc=512):
    G, T, C = xr.shape
    bw = C // N_GATE_BLOCKS
    tc = _tile(T, tc)
    tspec = pl.BlockSpec((None, tc, bw), lambda g, n, t: (g, t, n))
    vspec = pl.BlockSpec((1, bw), lambda g, n, t: (0, n))
    wspec = pl.BlockSpec((None, bw, bw), lambda g, n, t: (n, 0, 0))
    return pl.pallas_call(
        functools.partial(_rglru_prompt_body, tc=tc),
        grid=(G, N_GATE_BLOCKS, T // tc),
        in_specs=[tspec, tspec, pl.BlockSpec((CONV_W, bw), lambda g, n, t: (0, n)), vspec,
                  wspec, wspec, vspec, vspec, vspec],
        out_specs=[tspec, pl.BlockSpec((None, 1, bw), lambda g, n, t: (g, 0, n))],
        out_shape=[jax.ShapeDtypeStruct((G, T, C), BF16), jax.ShapeDtypeStruct((G, 1, C), F32)],
        scratch_shapes=[pltpu.VMEM((tc + SUBLANES, bw), F32), pltpu.VMEM((tc, bw), F32),
                        pltpu.VMEM((tc, bw), F32), pltpu.VMEM((SUBLANES, bw), F32)],
        compiler_params=_cp(("arbitrary", "arbitrary", "arbitrary")),
        name="rglru_prompt",
    )(xr, y, conv_w, conv_b.reshape(1, C), w_a, w_i, b_a.reshape(1, C), b_i.reshape(1, C),
      lam.reshape(1, C))


def _rglru_sample_body(xr_ref, y_ref, s0_ref, s1_ref, s2_ref, h0_ref, cw_ref, cb_ref, wa_ref, wi_ref,
                       ba_ref, bi_ref, lam_ref, hy_ref, h_ref):
    xc = (cb_ref[...] + cw_ref[0:1, :] * s0_ref[...] + cw_ref[1:2, :] * s1_ref[...]
          + cw_ref[2:3, :] * s2_ref[...] + cw_ref[3:4, :] * xr_ref[...])
    a, mult, gi = _rg_gates(xc, wa_ref[...], wi_ref[...], ba_ref[...], bi_ref[...], lam_ref[...])
    h = a * h0_ref[...] + mult * gi * xc
    h_ref[...] = h
    hy_ref[...] = (h * y_ref[...]).astype(hy_ref.dtype)


def rglru_sample(xr, y, state, h0, conv_w, conv_b, w_a, w_i, b_a, b_i, lam):
    _, B, C = xr.shape
    bw = C // N_GATE_BLOCKS
    st = state.reshape(B, (CONV_W - 1) * C)
    tspec = pl.BlockSpec((None, B, bw), lambda n: (0, 0, n))
    sspec = [pl.BlockSpec((B, bw), lambda n, k=k: (0, k * N_GATE_BLOCKS + n)) for k in range(CONV_W - 1)]
    vspec = pl.BlockSpec((1, bw), lambda n: (0, n))
    wspec = pl.BlockSpec((None, bw, bw), lambda n: (n, 0, 0))
    hspec = pl.BlockSpec((B, bw), lambda n: (0, n))
    return pl.pallas_call(
        _rglru_sample_body,
        grid=(N_GATE_BLOCKS,),
        in_specs=[tspec, tspec] + sspec + [hspec, pl.BlockSpec((CONV_W, bw), lambda n: (0, n)), vspec,
                                           wspec, wspec, vspec, vspec, vspec],
        out_specs=[tspec, hspec],
        out_shape=[jax.ShapeDtypeStruct((1, B, C), BF16), jax.ShapeDtypeStruct((B, C), F32)],
        compiler_params=_cp(("arbitrary",)),
        name="rglru_sample",
    )(xr, y, st, st, st, h0, conv_w, conv_b.reshape(1, C), w_a, w_i, b_a.reshape(1, C),
      b_i.reshape(1, C), lam.reshape(1, C))


def _rope128(pe, cos, sin):
    return pe * cos + pltpu.roll(pe, QK_ROPE, 1) * sin


def _kv_body(x_ref, w_ref, g_ref, cos_ref, sin_ref, ckv_ref, kpe_ref, wbf):
    @pl.when((pl.program_id(0) == 0) & (pl.program_id(1) == 0))
    def _():
        wbf[...] = w_ref[...].astype(BF16)

    acc = jnp.dot(x_ref[...], wbf[...], preferred_element_type=F32)
    c = acc[:, :KV_LORA]
    ckv_ref[...] = c * lax.rsqrt(jnp.mean(c * c, axis=-1, keepdims=True) + EPS) * g_ref[...]
    rot = _rope128(acc[:, KV_LORA:], cos_ref[...], sin_ref[...])
    kpe_ref[...] = rot[:, :QK_ROPE]


def kv_latent(hn, w_ext, g_ckv, cos_t, sin_t, *, tm=512):
    G, T, D = hn.shape
    tm = _tile(T, tm)
    NE = w_ext.shape[-1]
    if cos_t.shape[0] == 1:
        tab = pl.BlockSpec((1, LANES), lambda g, i: (0, 0))
    else:
        tab = pl.BlockSpec((tm, LANES), lambda g, i: (i, 0))
    return pl.pallas_call(
        _kv_body, grid=(G, T // tm),
        in_specs=[pl.BlockSpec((None, tm, D), lambda g, i: (g, i, 0)),
                  pl.BlockSpec((D, NE), lambda g, i: (0, 0)),
                  pl.BlockSpec((1, KV_LORA), lambda g, i: (0, 0)), tab, tab],
        out_specs=[pl.BlockSpec((None, tm, KV_LORA), lambda g, i: (g, i, 0)),
                   pl.BlockSpec((None, tm, QK_ROPE), lambda g, i: (g, i, 0))],
        out_shape=[jax.ShapeDtypeStruct((G, T, KV_LORA), F32), jax.ShapeDtypeStruct((G, T, QK_ROPE), F32)],
        scratch_shapes=[pltpu.VMEM((D, NE), BF16)],
        compiler_params=_cp(("arbitrary", "arbitrary")), name="kv_latent",
    )(hn, w_ext, g_ckv.reshape(1, KV_LORA), cos_t, sin_t)


def _kv_heads_body(c_ref, pe_ref, wk_ref, wv_ref, gn_ref, gp_ref, k_ref, v_ref, wk_bf, wv_bf):
    @pl.when((pl.program_id(0) == 0) & (pl.program_id(1) == 0))
    def _():
        wk_bf[...] = wk_ref[...].astype(BF16)
        wv_bf[...] = wv_ref[...].astype(BF16)

    c = c_ref[...].astype(BF16)
    pe = pe_ref[...]
    sspe = jnp.sum(pe * pe, axis=-1, keepdims=True)
    for h in range(N_HEADS):
        kh = jnp.dot(c, wk_bf[:, h * QK_NOPE:(h + 1) * QK_NOPE], preferred_element_type=F32)
        ss = jnp.sum(kh * kh, axis=-1, keepdims=True) + sspe
        inv = lax.rsqrt(ss * (1.0 / QK_HEAD) + EPS)
        k_ref[h, :, 0:QK_NOPE] = (kh * inv * gn_ref[...]).astype(k_ref.dtype)
        k_ref[h, :, QK_NOPE:QK_HEAD] = (pe * inv * gp_ref[...]).astype(k_ref.dtype)
        v_ref[h] = jnp.dot(c, wv_bf[:, h * V_HEAD:(h + 1) * V_HEAD],
                           preferred_element_type=F32).astype(v_ref.dtype)


def kv_heads(ckv, kpe, w_uk, w_uv, g_k, *, tm=512):
    G, T, L = ckv.shape
    tm = _tile(T, tm)
    HN = N_HEADS * QK_NOPE
    HV = N_HEADS * V_HEAD
    return pl.pallas_call(
        _kv_heads_body, grid=(G, T // tm),
        in_specs=[pl.BlockSpec((None, tm, L), lambda g, i: (g, i, 0)),
                  pl.BlockSpec((None, tm, QK_ROPE), lambda g, i: (g, i, 0)),
                  pl.BlockSpec((L, HN), lambda g, i: (0, 0)),
                  pl.BlockSpec((L, HV), lambda g, i: (0, 0)),
                  pl.BlockSpec((1, QK_NOPE), lambda g, i: (0, 0)),
                  pl.BlockSpec((1, QK_ROPE), lambda g, i: (0, 0))],
        out_specs=[pl.BlockSpec((None, N_HEADS, tm, QK_HEAD), lambda g, i: (g, 0, i, 0)),
                   pl.BlockSpec((None, N_HEADS, tm, V_HEAD), lambda g, i: (g, 0, i, 0))],
        out_shape=[jax.ShapeDtypeStruct((G, N_HEADS, T, QK_HEAD), BF16),
                   jax.ShapeDtypeStruct((G, N_HEADS, T, V_HEAD), BF16)],
        scratch_shapes=[pltpu.VMEM((L, HN), BF16), pltpu.VMEM((L, HV), BF16)],
        compiler_params=_cp(("arbitrary", "arbitrary")), name="kv_heads",
    )(ckv, kpe, w_uk.reshape(L, HN), w_uv.reshape(L, HV), g_k[:QK_NOPE].reshape(1, QK_NOPE),
      g_k[QK_NOPE:].reshape(1, QK_ROPE))


def _q_body(cq_ref, w_ref, cos_ref, sin_ref, gn_ref, gp_ref, q_ref, wbf):
    @pl.when((pl.program_id(0) == 0) & (pl.program_id(1) == 0))
    def _():
        wbf[...] = w_ref[...].astype(BF16)

    cq = cq_ref[...]
    cos = cos_ref[...]
    sin = sin_ref[...]
    hw = QK_NOPE + LANES
    for h in range(N_HEADS):
        a = jnp.dot(cq, wbf[:, h * hw:(h + 1) * hw], preferred_element_type=F32)
        nope = a[:, :QK_NOPE]
        rot = _rope128(a[:, QK_NOPE:], cos, sin)
        ss = jnp.sum(nope * nope, axis=-1, keepdims=True) + jnp.sum(rot * rot, axis=-1, keepdims=True)
        inv = lax.rsqrt(ss * (1.0 / QK_HEAD) + EPS) * SCALE
        q_ref[h, :, 0:QK_NOPE] = (nope * inv * gn_ref[...]).astype(q_ref.dtype)
        q_ref[h, :, QK_NOPE:QK_HEAD] = (rot * inv * gp_ref[...])[:, :QK_ROPE].astype(q_ref.dtype)


def q_heads(cq, w_ext, cos_t, sin_t, g_q, out_dtype, *, tm=512):
    G, T, L = cq.shape
    tm = _tile(T, tm)
    NE = w_ext.shape[-1]
    if cos_t.shape[0] == 1:
        tab = pl.BlockSpec((1, LANES), lambda g, i: (0, 0))
    else:
        tab = pl.BlockSpec((tm, LANES), lambda g, i: (i, 0))
    gp = jnp.zeros((1, LANES), F32).at[0, :QK_ROPE].set(g_q[QK_NOPE:])
    return pl.pallas_call(
        _q_body, grid=(G, T // tm),
        in_specs=[pl.BlockSpec((None, tm, L), lambda g, i: (g, i, 0)),
                  pl.BlockSpec((L, NE), lambda g, i: (0, 0)), tab, tab,
                  pl.BlockSpec((1, QK_NOPE), lambda g, i: (0, 0)),
                  pl.BlockSpec((1, LANES), lambda g, i: (0, 0))],
        out_specs=pl.BlockSpec((None, N_HEADS, tm, QK_HEAD), lambda g, i: (g, 0, i, 0)),
        out_shape=jax.ShapeDtypeStruct((G, N_HEADS, T, QK_HEAD), out_dtype),
        scratch_shapes=[pltpu.VMEM((L, NE), BF16)],
        compiler_params=_cp(("arbitrary", "arbitrary")), name="q_heads",
    )(cq, w_ext, cos_t, sin_t, g_q[:QK_NOPE].reshape(1, QK_NOPE), gp)


def _flash_body(q_ref, k_ref, v_ref, o_ref, *, tb):
    i = pl.program_id(2)
    qa = q_ref[0:tb, :]
    qb = q_ref[tb:2 * tb, :]

    def kv_block(j):
        r0 = pl.multiple_of(j * tb, tb)
        return k_ref[pl.ds(r0, tb), :], v_ref[pl.ds(r0, tb), :]

    def scores(q, k, masked):
        s = lax.dot_general(q, k, NT_DIMS, preferred_element_type=F32)
        if masked:
            row = lax.broadcasted_iota(jnp.int32, (tb, tb), 0)
            col = lax.broadcasted_iota(jnp.int32, (tb, tb), 1)
            s = jnp.where(col <= row, s, -jnp.inf)
        return s

    def update(s, v, carry):
        m, l, acc = carry
        m_new = jnp.maximum(m, jnp.max(s, axis=-1, keepdims=True))
        p = jnp.exp(s - m_new)
        corr = jnp.exp(m - m_new)
        l = l * corr + jnp.sum(p, axis=-1, keepdims=True)
        acc = acc * corr + jnp.dot(p.astype(BF16), v, preferred_element_type=F32)
        return m_new, l, acc

    def both(j, carries, masked_a):
        k, v = kv_block(j)
        sa = scores(qa, k, masked_a)
        sb = scores(qb, k, False)
        return update(sa, v, carries[0]), update(sb, v, carries[1])

    init = (jnp.full((tb, 1), -jnp.inf, F32), jnp.zeros((tb, 1), F32), jnp.zeros((tb, V_HEAD), F32))
    carries = lax.fori_loop(0, 2 * i, lambda j, c: both(j, c, False), (init, init))
    ca, cb = both(2 * i, carries, True)
    k, v = kv_block(2 * i + 1)
    cb = update(scores(qb, k, True), v, cb)
    o_ref[0:tb, :] = (ca[2] / ca[1]).astype(o_ref.dtype)
    o_ref[tb:2 * tb, :] = (cb[2] / cb[1]).astype(o_ref.dtype)


def flash_prompt(q, k, v, *, tb=512):
    G, H, T, E = q.shape
    tb = tb if T % (2 * tb) == 0 else T // 2
    tq = 2 * tb
    return pl.pallas_call(
        functools.partial(_flash_body, tb=tb),
        grid=(G, H, T // tq),
        in_specs=[pl.BlockSpec((None, None, tq, E), lambda g, h, i: (g, h, i, 0)),
                  pl.BlockSpec((None, None, T, E), lambda g, h, i: (g, h, 0, 0)),
                  pl.BlockSpec((None, None, T, V_HEAD), lambda g, h, i: (g, h, 0, 0))],
        out_specs=pl.BlockSpec((None, tq, V_HEAD), lambda g, h, i: (g, i, h)),
        out_shape=jax.ShapeDtypeStruct((G, T, H * V_HEAD), BF16),
        compiler_params=_cp(("arbitrary", "arbitrary", "arbitrary")), name="flash_prompt",
    )(q, k, v)


def _absorb_body(q_ref, k_ref, w_ref, gn_ref, gp_ref, qlat_ref, qpe_ref, sn_ref):
    q = q_ref[...]
    qn = (q[:, :QK_NOPE] * gn_ref[...]).astype(BF16)
    qlat_ref[...] = lax.dot_general(qn, w_ref[...].astype(BF16), NT_DIMS,
                                    preferred_element_type=F32).astype(qlat_ref.dtype)
    qpe_ref[...] = (q[:, QK_NOPE:] * gp_ref[...]).astype(qpe_ref.dtype)
    s = jnp.sum(q.astype(BF16).astype(F32) * k_ref[...].astype(F32), axis=-1, keepdims=True)
    sn_ref[...] = jnp.broadcast_to(s, sn_ref.shape)


def absorb_queries(q, k, w_uk, g_k):
    _, H, B, E = q.shape
    L = w_uk.shape[0]
    return pl.pallas_call(
        _absorb_body, grid=(H,),
        in_specs=[pl.BlockSpec((None, None, B, E), lambda h: (0, h, 0, 0)),
                  pl.BlockSpec((None, None, B, E), lambda h: (0, h, 0, 0)),
                  pl.BlockSpec((L, QK_NOPE), lambda h: (0, h)),
                  pl.BlockSpec((1, QK_NOPE), lambda h: (0, 0)),
                  pl.BlockSpec((1, QK_ROPE), lambda h: (0, 0))],
        out_specs=[pl.BlockSpec((None, B, L), lambda h: (h, 0, 0)),
                   pl.BlockSpec((None, B, QK_ROPE), lambda h: (h, 0, 0)),
                   pl.BlockSpec((None, B, LANES), lambda h: (h, 0, 0))],
        out_shape=[jax.ShapeDtypeStruct((H, B, L), BF16), jax.ShapeDtypeStruct((H, B, QK_ROPE), BF16),
                   jax.ShapeDtypeStruct((H, B, LANES), F32)],
        compiler_params=_cp(("arbitrary",)), name="absorb_queries",
    )(q, k, w_uk.reshape(L, H * QK_NOPE), g_k[:QK_NOPE].reshape(1, QK_NOPE),
      g_k[QK_NOPE:].reshape(1, QK_ROPE))


PAGES_PER_HALF = 16


def _page_copies(pt_ref, ck_hbm, kp_hbm, cbuf, kbuf, sem_c, sem_k, n, first, pp):
    copies = []
    for k in range(pp):
        page = pt_ref[n, first + k]
        copies.append(pltpu.make_async_copy(ck_hbm.at[page], cbuf.at[k], sem_c))
        copies.append(pltpu.make_async_copy(kp_hbm.at[page], kbuf.at[k], sem_k))
    return copies


def _paged_scores(waug, qpe, cbuf, kbuf):
    pp, pg, L = cbuf.shape
    HN = N_HEADS * QK_NOPE
    P = pp * pg
    cc = cbuf[...].reshape(P, L).astype(BF16)
    kpt = jnp.concatenate([kbuf[k] for k in range(pp)], axis=1)
    kt = lax.dot_general(waug[...], cc, NT_DIMS, preferred_element_type=F32)
    kn = kt[0:HN, :].reshape(N_HEADS, QK_NOPE, P)
    ssn = jnp.sum(kn * kn, axis=1)
    s_lat = kt[HN:HN + N_HEADS, :]
    sspe = jnp.sum(kpt * kpt, axis=0, keepdims=True)
    s_pe = jnp.dot(qpe, kpt.astype(BF16), preferred_element_type=F32)
    inv = lax.rsqrt((ssn + sspe) * (1.0 / QK_HEAD) + EPS)
    return (s_lat + s_pe) * inv, cc


def _paged_accumulate(s, cc, m_old, l_old, acc_old):
    m_new = jnp.maximum(m_old, jnp.max(s, axis=-1, keepdims=True))
    pexp = jnp.exp(s - m_new[:, 0:1])
    corr = jnp.exp(m_old - m_new)
    l_new = l_old * corr + jnp.sum(pexp, axis=-1, keepdims=True)
    acc_new = acc_old * corr[:, 0:1] + jnp.dot(pexp.astype(BF16), cc, preferred_element_type=F32)
    return m_new, l_new, acc_new


def _paged_body(pt_ref, qlat_ref, qpe_ref, sn_ref, cnew_ref, wt_ref, ck_hbm, kp_hbm, o_ref,
                waug, m_s, l_s, acc_s, cbuf_a, kbuf_a, cbuf_b, kbuf_b, sems, *, pp):
    n = pl.program_id(0)
    p = pl.program_id(1)
    steps = pl.num_programs(1)
    t = n * steps + p
    last_t = pl.num_programs(0) * steps - 1
    HN = N_HEADS * QK_NOPE

    def copies_a(nn, ps):
        return _page_copies(pt_ref, ck_hbm, kp_hbm, cbuf_a, kbuf_a, sems.at[0], sems.at[1], nn, 2 * ps * pp, pp)

    def copies_b(nn, ps):
        return _page_copies(pt_ref, ck_hbm, kp_hbm, cbuf_b, kbuf_b, sems.at[2], sems.at[3], nn,
                            (2 * ps + 1) * pp, pp)

    @pl.when(t == 0)
    def _():
        waug[0:HN, :] = wt_ref[...]
        for c in copies_a(n, p):
            c.start()

    for c in copies_b(n, p):
        c.start()

    @pl.when(p == 0)
    def _():
        waug[HN:HN + N_HEADS, :] = qlat_ref[...]
        m_s[...] = sn_ref[...]
        l_s[...] = jnp.ones(l_s.shape, F32)
        acc_s[...] = jnp.broadcast_to(cnew_ref[...], acc_s.shape)

    qpe = qpe_ref[...]
    state = (m_s[...], l_s[...], acc_s[...])
    for c in copies_a(n, p):
        c.wait()
    s_a, cc_a = _paged_scores(waug, qpe, cbuf_a, kbuf_a)

    nxt = jnp.minimum(t + 1, last_t)
    n_nxt = nxt // steps
    p_nxt = nxt - n_nxt * steps
    for c in copies_a(n_nxt, p_nxt):
        c.start()

    for c in copies_b(n, p):
        c.wait()
    s_b, cc_b = _paged_scores(waug, qpe, cbuf_b, kbuf_b)
    state = _paged_accumulate(s_a, cc_a, *state)
    m, l, acc = _paged_accumulate(s_b, cc_b, *state)
    m_s[...] = m
    l_s[...] = l
    acc_s[...] = acc

    @pl.when(t == last_t)
    def _():
        for c in copies_a(n_nxt, p_nxt):
            c.wait()

    @pl.when(p == steps - 1)
    def _():
        o_ref[...] = acc / l[:, 0:1]


def paged_attention(page_table, cache_ckv, cache_kpe_t, qlat, qpe, sn, c_new, w_uk_t):
    B, NP = page_table.shape
    L = cache_ckv.shape[-1]
    pp = next(c for c in (PAGES_PER_HALF, 8, 4, 2, 1) if c <= PAGES_PER_HALF and NP % (2 * c) == 0)
    HN = N_HEADS * QK_NOPE
    per_seq = lambda w: pl.BlockSpec((None, N_HEADS, w), lambda n, p, pt: (n, 0, 0))
    grid_spec = pltpu.PrefetchScalarGridSpec(
        num_scalar_prefetch=1,
        grid=(B, NP // (2 * pp)),
        in_specs=[per_seq(L), per_seq(QK_ROPE), per_seq(LANES),
                  pl.BlockSpec((None, 1, L), lambda n, p, pt: (n, 0, 0)),
                  pl.BlockSpec((HN, L), lambda n, p, pt: (0, 0)),
                  pl.BlockSpec(memory_space=pl.ANY), pl.BlockSpec(memory_space=pl.ANY)],
        out_specs=per_seq(L),
        scratch_shapes=[pltpu.VMEM((HN + N_HEADS, L), BF16), pltpu.VMEM((N_HEADS, LANES), F32),
                        pltpu.VMEM((N_HEADS, LANES), F32), pltpu.VMEM((N_HEADS, L), F32),
                        pltpu.VMEM((pp, PAGE, L), F32), pltpu.VMEM((pp, QK_ROPE, PAGE), F32),
                        pltpu.VMEM((pp, PAGE, L), F32), pltpu.VMEM((pp, QK_ROPE, PAGE), F32),
                        pltpu.SemaphoreType.DMA((4,))],
    )
    return pl.pallas_call(
        functools.partial(_paged_body, pp=pp),
        grid_spec=grid_spec,
        out_shape=jax.ShapeDtypeStruct((B, N_HEADS, L), F32),
        compiler_params=_cp(("arbitrary", "arbitrary")), name="paged_attention",
    )(page_table, qlat, qpe, sn, c_new, w_uk_t, cache_ckv, cache_kpe_t)


def _head_proj_body(x_ref, w_ref, o_ref):
    o_ref[...] = jnp.dot(x_ref[...].astype(BF16), w_ref[...].astype(BF16),
                         preferred_element_type=F32).astype(o_ref.dtype)


def head_value_proj(o_lat, w_uv):
    H, B, L = o_lat.shape
    return pl.pallas_call(
        _head_proj_body, grid=(H,),
        in_specs=[pl.BlockSpec((None, B, L), lambda h: (h, 0, 0)),
                  pl.BlockSpec((L, V_HEAD), lambda h: (0, h))],
        out_specs=pl.BlockSpec((None, B, V_HEAD), lambda h: (0, 0, h)),
        out_shape=jax.ShapeDtypeStruct((1, B, H * V_HEAD), BF16),
        compiler_params=_cp(("arbitrary",)), name="head_value_proj",
    )(o_lat, w_uv)


def _new_expert(te_ref, r):
    prev = te_ref[jnp.maximum(r - 1, 0)]
    return (r == 0) | (te_ref[r] != prev)


def _moe_up_body(te_ref, tv_ref, x_ref, wg_ref, wu_ref, o_ref, wg_bf, wu_bf):
    r = pl.program_id(1)

    @pl.when(_new_expert(te_ref, r))
    def _():
        wg_bf[...] = wg_ref[...].astype(BF16)
        wu_bf[...] = wu_ref[...].astype(BF16)

    @pl.when(tv_ref[r] > 0)
    def _():
        x = x_ref[...]
        a = jnp.dot(x, wg_bf[...], preferred_element_type=F32)
        u = jnp.dot(x, wu_bf[...], preferred_element_type=F32)
        o_ref[...] = (a * jax.nn.sigmoid(a) * u).astype(o_ref.dtype)

    @pl.when(tv_ref[r] == 0)
    def _():
        o_ref[...] = jnp.zeros(o_ref.shape, o_ref.dtype)


def _moe_down_body(te_ref, tv_ref, h_ref, wd_ref, o_ref, wd_bf):
    r = pl.program_id(1)

    @pl.when(_new_expert(te_ref, r))
    def _():
        wd_bf[...] = wd_ref[...].astype(BF16)

    @pl.when(tv_ref[r] > 0)
    def _():
        o_ref[...] = jnp.dot(h_ref[...], wd_bf[...], preferred_element_type=F32)

    @pl.when(tv_ref[r] == 0)
    def _():
        o_ref[...] = jnp.zeros(o_ref.shape, o_ref.dtype)


def moe_experts(xg, tile_expert, tile_valid, w_gate, w_up, w_down, *, tm, tf=512, tn=512):
    R, D = xg.shape
    F = w_gate.shape[-1]
    tf = _tile(F, tf)
    tn = _tile(D, tn)
    up_spec = pltpu.PrefetchScalarGridSpec(
        num_scalar_prefetch=2, grid=(F // tf, R // tm),
        in_specs=[pl.BlockSpec((tm, D), lambda f, r, te, tv: (r, 0)),
                  pl.BlockSpec((None, D, tf), lambda f, r, te, tv: (te[r], 0, f)),
                  pl.BlockSpec((None, D, tf), lambda f, r, te, tv: (te[r], 0, f))],
        out_specs=pl.BlockSpec((tm, tf), lambda f, r, te, tv: (r, f)),
        scratch_shapes=[pltpu.VMEM((D, tf), BF16), pltpu.VMEM((D, tf), BF16)])
    h = pl.pallas_call(
        _moe_up_body, grid_spec=up_spec, out_shape=jax.ShapeDtypeStruct((R, F), BF16),
        compiler_params=_cp(("arbitrary", "arbitrary")), name="moe_up",
    )(tile_expert, tile_valid, xg, w_gate, w_up)
    down_spec = pltpu.PrefetchScalarGridSpec(
        num_scalar_prefetch=2, grid=(D // tn, R // tm),
        in_specs=[pl.BlockSpec((tm, F), lambda j, r, te, tv: (r, 0)),
                  pl.BlockSpec((None, F, tn), lambda j, r, te, tv: (te[r], 0, j))],
        out_specs=pl.BlockSpec((tm, tn), lambda j, r, te, tv: (r, j)),
        scratch_shapes=[pltpu.VMEM((F, tn), BF16)])
    return pl.pallas_call(
        _moe_down_body, grid_spec=down_spec, out_shape=jax.ShapeDtypeStruct((R, D), F32),
        compiler_params=_cp(("arbitrary", "arbitrary")), name="moe_down",
    )(tile_expert, tile_valid, h, w_down)


def _combine_body(x_ref, g_ref, ya_ref, yb_ref, w_ref, o_ref):
    w = w_ref[...]
    f = w[:, 0:1] * ya_ref[...] + w[:, 1:2] * yb_ref[...]
    o_ref[...] = x_ref[...] + g_ref[...] * f


def moe_combine(x, gate_mod, gate_chunk, ya, yb, w, *, tm=256):
    G, T, D = x.shape
    R = gate_mod.shape[1]
    tm = _tile(T, tm)
    tspec = pl.BlockSpec((None, tm, D), lambda g, i: (g, i, 0))
    return pl.pallas_call(
        _combine_body, grid=(G, T // tm),
        in_specs=[tspec, pl.BlockSpec((None, R, D), lambda g, i: (g, 0, gate_chunk)), tspec, tspec,
                  pl.BlockSpec((None, tm, LANES), lambda g, i: (g, i, 0))],
        out_specs=tspec, out_shape=jax.ShapeDtypeStruct((G, T, D), F32),
        compiler_params=_cp(("arbitrary", "arbitrary")), name="moe_combine",
    )(x, gate_mod, ya, yb, w)


def _moe_layout(ids, tm):
    A = ids.shape[0]
    R = -(-(A + N_EXPERTS * (tm - 1)) // tm) * tm
    onehot = (ids[:, None] == jnp.arange(N_EXPERTS, dtype=jnp.int32)[None, :]).astype(jnp.int32)
    rank = jnp.cumsum(onehot, axis=0) - onehot
    counts = jnp.sum(onehot, axis=0)
    tiles = (counts + tm - 1) // tm
    tile_end = jnp.cumsum(tiles)
    row_off = (tile_end - tiles) * tm
    pos = jnp.sum(onehot * (row_off[None, :] + rank), axis=1)
    tile_id = jnp.arange(R // tm, dtype=jnp.int32)
    tile_expert = jnp.minimum(jnp.sum((tile_id[:, None] >= tile_end[None, :]).astype(jnp.int32), axis=1),
                              N_EXPERTS - 1).astype(jnp.int32)
    tile_valid = (tile_id < tile_end[-1]).astype(jnp.int32)
    return R, pos.astype(jnp.int32), tile_expert, tile_valid


def _rope_tables(pos):
    half = QK_ROPE // 2
    inv = ROPE_THETA ** (-jnp.arange(half, dtype=F32) / half)
    ang = pos.astype(F32)[:, None] * inv[None, :]
    cos, sin = jnp.cos(ang), jnp.sin(ang)
    z = jnp.zeros((pos.shape[0], LANES - QK_ROPE), F32)
    return jnp.concatenate([cos, cos, z], axis=-1), jnp.concatenate([-sin, sin, z], axis=-1)


def _swap_halves(w):
    half = QK_ROPE // 2
    return jnp.concatenate([w[..., half:], w[..., :half]], axis=-1)


def kernel(x_prompt, x_sample, state_rglru_h, state_rglru_conv, cache_ckv, cache_kpe, page_table, c_prompt, c_sample, w_ada, b_ada, g_norm, rg_w_y, rg_b_y, rg_w_x, rg_b_x, rg_conv_w, rg_conv_b, rg_w_a, rg_b_a, rg_w_i, rg_b_i, rg_lam, rg_w_out, rg_b_out, kv_w_ada, kv_b_ada, kv_g_in, kv_w_dkv, kv_g_ckv, kv_w_uk, kv_w_uv, kv_g_k, mla_w_dq, mla_g_cq, mla_w_uq, mla_g_q, mla_w_o, ffn_w_gate, ffn_w_up, ffn_w_down, moe_w_router, moe_w_gate, moe_w_up, moe_w_down):
    Bp, S, D = x_prompt.shape
    Bs = x_sample.shape[0]
    past = page_table.shape[1] * PAGE
    HN = N_HEADS * QK_NOPE
    HV = N_HEADS * V_HEAD

    xs = {"p": x_prompt, "s": x_sample.reshape(1, Bs, D)}
    groups = ("p", "s")

    pad = (-(Bs + Bp)) % 16
    c_all = jnp.concatenate([c_sample, c_prompt, jnp.zeros((pad, D), F32)], axis=0)[None]

    def split_mods(m):
        return {"p": m[0, Bs:Bs + Bp].reshape(Bp, 1, -1), "s": m[:, :Bs]}

    mods = [split_mods(mm(c_all, w_ada, (l,), bias=b_ada[l], pre_silu=True, name="adaln"))
            for l in range(2)]
    mods_kv = split_mods(mm(c_all, kv_w_ada, bias=kv_b_ada, pre_silu=True, name="adaln_kv"))

    x1 = {}
    h_last = {}
    conv_last = {}
    for gname in groups:
        x = xs[gname]
        md = mods[0][gname]
        xn = norm_mod(x, g_norm[0, 0], md, 0, 1)
        y = mm(xn, rg_w_y, (0,), bias=rg_b_y[0], act="gelu", name="rg_y")
        xr = mm(xn, rg_w_x, (0,), bias=rg_b_x[0], name="rg_x")
        if gname == "p":
            hy, hl = rglru_prompt(xr, y, rg_conv_w[0], rg_conv_b[0], rg_w_a[0], rg_w_i[0], rg_b_a[0],
                                  rg_b_i[0], rg_lam[0])
            h_last[gname] = hl.reshape(1, Bp, D)
            conv_last[gname] = xr[:, S - (CONV_W - 1):, :][None]
        else:
            hy, hl = rglru_sample(xr, y, state_rglru_conv[0], state_rglru_h[0], rg_conv_w[0], rg_conv_b[0],
                                  rg_w_a[0], rg_w_i[0], rg_b_a[0], rg_b_i[0], rg_lam[0])
            h_last[gname] = hl[None]
            conv_last[gname] = jnp.concatenate([state_rglru_conv[0][:, 1:], xr.reshape(Bs, 1, D)], axis=1)[None]
        xa = mm(hy, rg_w_out, (0,), bias=rg_b_out[0], res=x, gate=md, gate_chunk=2, name="rg_out")
        xn = norm_mod(xa, g_norm[0, 1], md, 3, 4)
        hmid = swiglu_up(xn, ffn_w_gate, ffn_w_up, (0,))
        x1[gname] = mm(hmid, ffn_w_down, (0,), res=xa, gate=md, gate_chunk=5, name="ffn_down")

    w_dkv_ext = jnp.concatenate([kv_w_dkv, _swap_halves(kv_w_dkv[:, KV_LORA:])], axis=-1)
    w_uq = mla_w_uq[0]
    w_uq_ext = jnp.concatenate([w_uq, _swap_halves(w_uq[..., QK_NOPE:])], axis=-1).reshape(
        w_uq.shape[0], N_HEADS * (QK_NOPE + LANES))
    tabs = {"p": _rope_tables(jnp.arange(S)), "s": _rope_tables(jnp.full((1,), past))}
    w_uk2 = kv_w_uk.reshape(KV_LORA, HN)
    w_uv2 = kv_w_uv.reshape(KV_LORA, HV)

    ckv = {}
    kpe = {}
    xb = {}
    for gname in groups:
        x = x1[gname]
        md = mods[1][gname]
        cos_t, sin_t = tabs[gname]
        hn = norm_mod(x, kv_g_in, mods_kv[gname], 0, 1)
        ckv[gname], kpe[gname] = kv_latent(hn, w_dkv_ext, kv_g_ckv, cos_t, sin_t)
        k, v = kv_heads(ckv[gname], kpe[gname], w_uk2, w_uv2, kv_g_k)
        xn = norm_mod(x, g_norm[1, 0], md, 0, 1)
        cq = mm(xn, mla_w_dq, (0,), rms_g=mla_g_cq[0], out_dtype=BF16, name="q_down")
        if gname == "p":
            q = q_heads(cq, w_uq_ext, cos_t, sin_t, mla_g_q[0], BF16)
            o = flash_prompt(q, k, v)
        else:
            q = q_heads(cq, w_uq_ext, cos_t, sin_t, mla_g_q[0], F32)
            qlat, qpe, sn = absorb_queries(q, k, w_uk2, kv_g_k)
            o_lat = paged_attention(page_table, cache_ckv, jnp.swapaxes(cache_kpe, 1, 2), jnp.swapaxes(qlat, 0, 1),
                                    jnp.swapaxes(qpe, 0, 1), jnp.swapaxes(sn, 0, 1),
                                    ckv[gname].reshape(Bs, 1, KV_LORA), w_uk2.T.astype(BF16))
            o = head_value_proj(jnp.swapaxes(o_lat, 0, 1), w_uv2)
        xb[gname] = mm(o, mla_w_o, (0,), res=x, gate=md, gate_chunk=2, name="attn_out")

    xn_l, gate_l, idx_l = [], [], []
    for gname in groups:
        xn, gt, ix = norm_mod(xb[gname], g_norm[1, 1], mods[1][gname], 3, 4, router_w=moe_w_router[0])
        xn_l.append(xn.reshape(-1, D))
        gate_l.append(gt)
        idx_l.append(ix[..., :2].reshape(-1, 2))
    xn_all = jnp.concatenate(xn_l, axis=0)
    ids = jnp.concatenate(idx_l, axis=0).reshape(-1)
    tm_moe = 512
    R, pos, tile_expert, tile_valid = _moe_layout(ids, tm_moe)
    tok_of_row = jnp.zeros((R,), jnp.int32).at[pos].set(jnp.arange(ids.shape[0], dtype=jnp.int32) // 2)
    xg = xn_all.at[tok_of_row].get(mode="promise_in_bounds")
    yg = moe_experts(xg, tile_expert, tile_valid, moe_w_gate[0], moe_w_up[0], moe_w_down[0], tm=tm_moe)
    pos2 = pos.reshape(-1, 2)
    n_p = Bp * S
    out = {}
    for gi, gname in enumerate(groups):
        sl = slice(0, n_p) if gname == "p" else slice(n_p, n_p + Bs)
        shp = xb[gname].shape
        ya = yg.at[pos2[sl, 0]].get(mode="promise_in_bounds").reshape(shp)
        yb = yg.at[pos2[sl, 1]].get(mode="promise_in_bounds").reshape(shp)
        out[gname] = moe_combine(xb[gname], mods[1][gname], 5, ya, yb, gate_l[gi])

    return (out["p"], out["s"].reshape(Bs, 1, D),
            h_last["p"], conv_last["p"], ckv["p"], kpe["p"],
            h_last["s"], conv_last["s"], ckv["s"].reshape(Bs, 1, KV_LORA), kpe["s"].reshape(Bs, 1, QK_ROPE))
```

```python
import functools
import math

import jax
import jax.numpy as jnp
from jax import lax
from jax.experimental import pallas as pl
from jax.experimental.pallas import tpu as pltpu

F32 = jnp.float32
BF16 = jnp.bfloat16

N_GATE_BLOCKS = 8
CONV_W = 4
RG_C = 8.0
N_HEADS = 16
KV_LORA = 512
QK_NOPE = 128
QK_ROPE = 64
QK_HEAD = QK_NOPE + QK_ROPE
V_HEAD = 128
ROPE_THETA = 10000.0
SCALE = QK_HEAD ** -0.5
N_EXPERTS = 8
EPS = 1e-6
PAGE = 128
LANES = 128
SUBLANES = 8
VMEM_LIMIT = 56 * 1024 * 1024
NT_DIMS = (((1,), (1,)), ((), ()))


def _cp(sem):
    return pltpu.CompilerParams(dimension_semantics=sem, vmem_limit_bytes=VMEM_LIMIT)


def _tile(n, pref):
    return pref if n % pref == 0 else n


def _mm_body(*refs, act, has_bias, has_res, has_rms, pre_silu):
    x_ref, w_ref = refs[0], refs[1]
    k = 2
    b_ref = res_ref = gate_ref = g_ref = None
    if has_bias:
        b_ref = refs[k]; k += 1
    if has_res:
        res_ref, gate_ref = refs[k], refs[k + 1]; k += 2
    if has_rms:
        g_ref = refs[k]; k += 1
    o_ref, wbf = refs[k], refs[k + 1]

    @pl.when((pl.program_id(1) == 0) & (pl.program_id(2) == 0))
    def _():
        wbf[...] = w_ref[...].astype(BF16)

    x = x_ref[...]
    if pre_silu:
        x = x * jax.nn.sigmoid(x)
    acc = jnp.dot(x.astype(BF16), wbf[...], preferred_element_type=F32)
    if has_bias:
        acc = acc + b_ref[...]
    if act == "gelu":
        acc = jax.nn.gelu(acc)
    if has_rms:
        acc = acc * lax.rsqrt(jnp.mean(acc * acc, axis=-1, keepdims=True) + EPS) * g_ref[...]
    if has_res:
        acc = res_ref[...] + gate_ref[...] * acc
    o_ref[...] = acc.astype(o_ref.dtype)


def mm(x, w, widx=(), *, bias=None, act=None, res=None, gate=None, gate_chunk=0,
       rms_g=None, pre_silu=False, out_dtype=F32, tm=512, tn=512, name="mm"):
    G, T, K = x.shape
    N = w.shape[-1]
    tm = _tile(T, tm)
    tn = _tile(N, tn)
    nb = N // tn
    widx = tuple(widx)
    in_specs = [
        pl.BlockSpec((None, tm, K), lambda j, g, i: (g, i, 0)),
        pl.BlockSpec((None,) * len(widx) + (K, tn), lambda j, g, i: widx + (0, j)),
    ]
    args = [x, w]
    if bias is not None:
        in_specs.append(pl.BlockSpec((1, tn), lambda j, g, i: (0, j)))
        args.append(bias.reshape(1, N))
    if res is not None:
        R = gate.shape[1]
        in_specs.append(pl.BlockSpec((None, tm, tn), lambda j, g, i: (g, i, j)))
        in_specs.append(pl.BlockSpec((None, R, tn), lambda j, g, i: (g, 0, gate_chunk * nb + j)))
        args += [res, gate]
    if rms_g is not None:
        assert tn == N
        in_specs.append(pl.BlockSpec((1, tn), lambda j, g, i: (0, j)))
        args.append(rms_g.reshape(1, N))
    body = functools.partial(_mm_body, act=act, has_bias=bias is not None, has_res=res is not None,
                             has_rms=rms_g is not None, pre_silu=pre_silu)
    return pl.pallas_call(
        body,
        grid=(nb, G, T // tm),
        in_specs=in_specs,
        out_specs=pl.BlockSpec((None, tm, tn), lambda j, g, i: (g, i, j)),
        out_shape=jax.ShapeDtypeStruct((G, T, N), out_dtype),
        scratch_shapes=[pltpu.VMEM((K, tn), BF16)],
        compiler_params=_cp(("arbitrary", "arbitrary", "arbitrary")),
        name=name,
    )(*args)


def _swiglu_body(x_ref, wg_ref, wu_ref, o_ref, wg_bf, wu_bf):
    @pl.when((pl.program_id(1) == 0) & (pl.program_id(2) == 0))
    def _():
        wg_bf[...] = wg_ref[...].astype(BF16)
        wu_bf[...] = wu_ref[...].astype(BF16)

    x = x_ref[...]
    a = jnp.dot(x, wg_bf[...], preferred_element_type=F32)
    u = jnp.dot(x, wu_bf[...], preferred_element_type=F32)
    o_ref[...] = (a * jax.nn.sigmoid(a) * u).astype(o_ref.dtype)


def swiglu_up(x, wg, wu, widx, *, tm=512, tn=512):
    G, T, K = x.shape
    N = wg.shape[-1]
    tm = _tile(T, tm)
    tn = _tile(N, tn)
    widx = tuple(widx)
    wspec = pl.BlockSpec((None,) * len(widx) + (K, tn), lambda j, g, i: widx + (0, j))
    return pl.pallas_call(
        _swiglu_body,
        grid=(N // tn, G, T // tm),
        in_specs=[pl.BlockSpec((None, tm, K), lambda j, g, i: (g, i, 0)), wspec, wspec],
        out_specs=pl.BlockSpec((None, tm, tn), lambda j, g, i: (g, i, j)),
        out_shape=jax.ShapeDtypeStruct((G, T, N), BF16),
        scratch_shapes=[pltpu.VMEM((K, tn), BF16), pltpu.VMEM((K, tn), BF16)],
        compiler_params=_cp(("arbitrary", "arbitrary", "arbitrary")),
        name="swiglu_up",
    )(x, wg, wu)


def _dual_body(x_ref, w1_ref, w2_ref, b1_ref, b2_ref, o1_ref, o2_ref, w1_bf, w2_bf):
    @pl.when((pl.program_id(1) == 0) & (pl.program_id(2) == 0))
    def _():
        w1_bf[...] = w1_ref[...].astype(BF16)
        w2_bf[...] = w2_ref[...].astype(BF16)

    x = x_ref[...]
    o1_ref[...] = jax.nn.gelu(jnp.dot(x, w1_bf[...], preferred_element_type=F32) + b1_ref[...])
    o2_ref[...] = jnp.dot(x, w2_bf[...], preferred_element_type=F32) + b2_ref[...]


def gelu_and_linear(x, w1, w2, widx, b1, b2, *, tm=512, tn=512):
    G, T, K = x.shape
    N = w1.shape[-1]
    tm = _tile(T, tm)
    tn = _tile(N, tn)
    widx = tuple(widx)
    wspec = pl.BlockSpec((None,) * len(widx) + (K, tn), lambda j, g, i: widx + (0, j))
    bspec = pl.BlockSpec((1, tn), lambda j, g, i: (0, j))
    ospec = pl.BlockSpec((None, tm, tn), lambda j, g, i: (g, i, j))
    return pl.pallas_call(
        _dual_body,
        grid=(N // tn, G, T // tm),
        in_specs=[pl.BlockSpec((None, tm, K), lambda j, g, i: (g, i, 0)), wspec, wspec, bspec, bspec],
        out_specs=[ospec, ospec],
        out_shape=[jax.ShapeDtypeStruct((G, T, N), F32), jax.ShapeDtypeStruct((G, T, N), F32)],
        scratch_shapes=[pltpu.VMEM((K, tn), BF16), pltpu.VMEM((K, tn), BF16)],
        compiler_params=_cp(("arbitrary", "arbitrary", "arbitrary")),
        name="rg_in_proj",
    )(x, w1, w2, b1.reshape(1, N), b2.reshape(1, N))


def _modulated_norm(x, g, shift, scale):
    y = x * lax.rsqrt(jnp.mean(x * x, axis=-1, keepdims=True) + EPS)
    return (y * g) * (1.0 + scale) + shift


def _norm_body(x_ref, g_ref, sh_ref, sc_ref, o_ref):
    o_ref[...] = _modulated_norm(x_ref[...], g_ref[...], sh_ref[...], sc_ref[...]).astype(o_ref.dtype)


def _norm_router_body(x_ref, g_ref, sh_ref, sc_ref, wr_ref, o_ref, gate_ref, idx_ref):
    xn = _modulated_norm(x_ref[...], g_ref[...], sh_ref[...], sc_ref[...])
    o_ref[...] = xn.astype(o_ref.dtype)
    logits = jnp.dot(xn, wr_ref[...], preferred_element_type=F32, precision=lax.Precision.HIGHEST)
    lane_i = lax.broadcasted_iota(jnp.int32, logits.shape, 1)
    lane = lane_i.astype(F32)
    neg = jnp.float32(-jnp.inf)
    lg = jnp.where(lane_i < N_EXPERTS, logits, neg)
    m1 = jnp.max(lg, axis=-1, keepdims=True)
    i1 = jnp.min(jnp.where(lg == m1, lane, float(LANES)), axis=-1, keepdims=True)
    lg2 = jnp.where(lane == i1, neg, lg)
    m2 = jnp.max(lg2, axis=-1, keepdims=True)
    i2 = jnp.min(jnp.where(lg2 == m2, lane, float(LANES)), axis=-1, keepdims=True)
    e2 = jnp.exp(m2 - m1)
    g1 = 1.0 / (1.0 + e2)
    g2 = e2 * g1
    gate_ref[...] = jnp.where(lane_i == 0, g1, jnp.where(lane_i == 1, g2, 0.0))
    idx_ref[...] = jnp.where(lane_i == 0, i1, jnp.where(lane_i == 1, i2, 0.0)).astype(jnp.int32)


def norm_mod(x, gain, mods, k_shift, k_scale, *, router_w=None, tm=256):
    G, T, D = x.shape
    R = mods.shape[1]
    tm = _tile(T, tm)
    if R != 1:
        assert R == T and tm == T
    in_specs = [
        pl.BlockSpec((None, tm, D), lambda g, i: (g, i, 0)),
        pl.BlockSpec((1, D), lambda g, i: (0, 0)),
        pl.BlockSpec((None, R, D), lambda g, i: (g, 0, k_shift)),
        pl.BlockSpec((None, R, D), lambda g, i: (g, 0, k_scale)),
    ]
    args = [x, gain.reshape(1, D), mods, mods]
    xspec = pl.BlockSpec((None, tm, D), lambda g, i: (g, i, 0))
    if router_w is None:
        return pl.pallas_call(
            _norm_body, grid=(G, T // tm), in_specs=in_specs, out_specs=xspec,
            out_shape=jax.ShapeDtypeStruct((G, T, D), BF16),
            compiler_params=_cp(("arbitrary", "arbitrary")), name="norm_mod",
        )(*args)
    wr = jnp.zeros((D, LANES), F32).at[:, :N_EXPERTS].set(router_w)
    in_specs.append(pl.BlockSpec((D, LANES), lambda g, i: (0, 0)))
    lspec = pl.BlockSpec((None, tm, LANES), lambda g, i: (g, i, 0))
    return pl.pallas_call(
        _norm_router_body, grid=(G, T // tm), in_specs=in_specs,
        out_specs=[xspec, lspec, lspec],
        out_shape=[jax.ShapeDtypeStruct((G, T, D), F32),
                   jax.ShapeDtypeStruct((G, T, LANES), F32),
                   jax.ShapeDtypeStruct((G, T, LANES), jnp.int32)],
        compiler_params=_cp(("arbitrary", "arbitrary")), name="norm_router",
    )(*args, wr)


def _rg_gates(xc, wa, wi, ba, bi, lam):
    xb = xc.astype(BF16)
    r = 0.5 * jnp.tanh(0.5 * (jnp.dot(xb, wa.astype(BF16), preferred_element_type=F32) + ba)) + 0.5
    gi = 0.5 * jnp.tanh(0.5 * (jnp.dot(xb, wi.astype(BF16), preferred_element_type=F32) + bi)) + 0.5
    neg_lam = -lam
    softplus = jnp.maximum(neg_lam, 0.0) + jnp.log1p(jnp.exp(-jnp.abs(neg_lam)))
    log_a = -RG_C * r * softplus
    a = jnp.exp(log_a)
    th = jnp.tanh(log_a)
    mult = jnp.sqrt(-2.0 * th / (1.0 - th))
    return a, mult, gi


def _rglru_prompt_body(xr_ref, y_ref, cw_ref, cb_ref, wa_ref, wi_ref, ba_ref, bi_ref, lam_ref,
                       hy_ref, hl_ref, ext, a_s, b_s, h_s, *, tc):
    t = pl.program_id(2)
    C = xr_ref.shape[-1]

    @pl.when(t == 0)
    def _():
        ext[0:SUBLANES, :] = jnp.zeros((SUBLANES, C), F32)
        h_s[...] = jnp.zeros((SUBLANES, C), F32)

    ext[SUBLANES:SUBLANES + tc, :] = xr_ref[...]
    xc = cb_ref[...]
    for k in range(CONV_W):
        off = SUBLANES - (CONV_W - 1) + k
        xc = xc + cw_ref[k:k + 1, :] * ext[off:off + tc, :]
    ext[0:SUBLANES, :] = ext[tc:tc + SUBLANES, :]

    a, mult, gi = _rg_gates(xc, wa_ref[...], wi_ref[...], ba_ref[...], bi_ref[...], lam_ref[...])
    pos = t * tc + lax.broadcasted_iota(jnp.int32, (tc, 1), 0)
    mult = jnp.where(pos == 0, 1.0, mult)
    a_s[...] = a
    b_s[...] = mult * gi * xc

    ridx = lax.broadcasted_iota(jnp.int32, (SUBLANES, C), 0)

    def group(gidx, h):
        r0 = pl.multiple_of(gidx * SUBLANES, SUBLANES)
        a8 = a_s[pl.ds(r0, SUBLANES), :]
        b8 = b_s[pl.ds(r0, SUBLANES), :]
        for s in (1, 2, 4):
            keep = ridx >= s
            a_sh = jnp.where(keep, pltpu.roll(a8, s, 0), 1.0)
            b_sh = jnp.where(keep, pltpu.roll(b8, s, 0), 0.0)
            b8 = a8 * b_sh + b8
            a8 = a8 * a_sh
        h8 = a8 * h + b8
        hy_ref[pl.ds(r0, SUBLANES), :] = (h8 * y_ref[pl.ds(r0, SUBLANES), :]).astype(hy_ref.dtype)
        return jnp.broadcast_to(h8[SUBLANES - 1:SUBLANES, :], h8.shape)

    h = lax.fori_loop(0, tc // SUBLANES, group, h_s[...], unroll=4)
    h_s[...] = h
    hl_ref[...] = h[0:1, :]


def rglru_prompt(xr, y, conv_w, conv_b, w_a, w_i, b_a, b_i, lam, *, tc=512):
    G, T, C = xr.shape
    bw = C // N_GATE_BLOCKS
    tc = _tile(T, tc)
    tspec = pl.BlockSpec((None, tc, bw), lambda g, n, t: (g, t, n))
    vspec = pl.BlockSpec((1, bw), lambda g, n, t: (0, n))
    wspec = pl.BlockSpec((None, bw, bw), lambda g, n, t: (n, 0, 0))
    return pl.pallas_call(
        functools.partial(_rglru_prompt_body, tc=tc),
        grid=(G, N_GATE_BLOCKS, T // tc),
        in_specs=[tspec, tspec, pl.BlockSpec((CONV_W, bw), lambda g, n, t: (0, n)), vspec,
                  wspec, wspec, vspec, vspec, vspec],
        out_specs=[tspec, pl.BlockSpec((None, 1, bw), lambda g, n, t: (g, 0, n))],
        out_shape=[jax.ShapeDtypeStruct((G, T, C), BF16), jax.ShapeDtypeStruct((G, 1, C), F32)],
        scratch_shapes=[pltpu.VMEM((tc + SUBLANES, bw), F32), pltpu.VMEM((tc, bw), F32),
                        pltpu.VMEM((tc, bw), F32), pltpu.VMEM((SUBLANES, bw), F32)],
        compiler_params=_cp(("arbitrary", "arbitrary", "arbitrary")),
        name="rglru_prompt",
    )(xr, y, conv_w, conv_b.reshape(1, C), w_a, w_i, b_a.reshape(1, C), b_i.reshape(1, C),
      lam.reshape(1, C))


def _rglru_sample_body(xr_ref, y_ref, s0_ref, s1_ref, s2_ref, h0_ref, cw_ref, cb_ref, wa_ref, wi_ref,
                       ba_ref, bi_ref, lam_ref, hy_ref, h_ref):
    xc = (cb_ref[...] + cw_ref[0:1, :] * s0_ref[...] + cw_ref[1:2, :] * s1_ref[...]
          + cw_ref[2:3, :] * s2_ref[...] + cw_ref[3:4, :] * xr_ref[...])
    a, mult, gi = _rg_gates(xc, wa_ref[...], wi_ref[...], ba_ref[...], bi_ref[...], lam_ref[...])
    h = a * h0_ref[...] + mult * gi * xc
    h_ref[...] = h
    hy_ref[...] = (h * y_ref[...]).astype(hy_ref.dtype)


def rglru_sample(xr, y, state, h0, conv_w, conv_b, w_a, w_i, b_a, b_i, lam):
    _, B, C = xr.shape
    bw = C // N_GATE_BLOCKS
    st = state.reshape(B, (CONV_W - 1) * C)
    tspec = pl.BlockSpec((None, B, bw), lambda n: (0, 0, n))
    sspec = [pl.BlockSpec((B, bw), lambda n, k=k: (0, k * N_GATE_BLOCKS + n)) for k in range(CONV_W - 1)]
    vspec = pl.BlockSpec((1, bw), lambda n: (0, n))
    wspec = pl.BlockSpec((None, bw, bw), lambda n: (n, 0, 0))
    hspec = pl.BlockSpec((B, bw), lambda n: (0, n))
    return pl.pallas_call(
        _rglru_sample_body,
        grid=(N_GATE_BLOCKS,),
        in_specs=[tspec, tspec] + sspec + [hspec, pl.BlockSpec((CONV_W, bw), lambda n: (0, n)), vspec,
                                           wspec, wspec, vspec, vspec, vspec],
        out_specs=[tspec, hspec],
        out_shape=[jax.ShapeDtypeStruct((1, B, C), BF16), jax.ShapeDtypeStruct((B, C), F32)],
        compiler_params=_cp(("arbitrary",)),
        name="rglru_sample",
    )(xr, y, st, st, st, h0, conv_w, conv_b.reshape(1, C), w_a, w_i, b_a.reshape(1, C),
      b_i.reshape(1, C), lam.reshape(1, C))


def _rope128(pe, cos, sin):
    return pe * cos + pltpu.roll(pe, QK_ROPE, 1) * sin


def _kv_body(x_ref, w_ref, g_ref, cos_ref, sin_ref, ckv_ref, kpe_ref, wbf):
    @pl.when((pl.program_id(0) == 0) & (pl.program_id(1) == 0))
    def _():
        wbf[...] = w_ref[...].astype(BF16)

    acc = jnp.dot(x_ref[...], wbf[...], preferred_element_type=F32)
    c = acc[:, :KV_LORA]
    ckv_ref[...] = c * lax.rsqrt(jnp.mean(c * c, axis=-1, keepdims=True) + EPS) * g_ref[...]
    rot = _rope128(acc[:, KV_LORA:], cos_ref[...], sin_ref[...])
    kpe_ref[...] = rot[:, :QK_ROPE]


def kv_latent(hn, w_ext, g_ckv, cos_t, sin_t, *, tm=512):
    G, T, D = hn.shape
    tm = _tile(T, tm)
    NE = w_ext.shape[-1]
    if cos_t.shape[0] == 1:
        tab = pl.BlockSpec((1, LANES), lambda g, i: (0, 0))
    else:
        tab = pl.BlockSpec((tm, LANES), lambda g, i: (i, 0))
    return pl.pallas_call(
        _kv_body, grid=(G, T // tm),
        in_specs=[pl.BlockSpec((None, tm, D), lambda g, i: (g, i, 0)),
                  pl.BlockSpec((D, NE), lambda g, i: (0, 0)),
                  pl.BlockSpec((1, KV_LORA), lambda g, i: (0, 0)), tab, tab],
        out_specs=[pl.BlockSpec((None, tm, KV_LORA), lambda g, i: (g, i, 0)),
                   pl.BlockSpec((None, tm, QK_ROPE), lambda g, i: (g, i, 0))],
        out_shape=[jax.ShapeDtypeStruct((G, T, KV_LORA), F32), jax.ShapeDtypeStruct((G, T, QK_ROPE), F32)],
        scratch_shapes=[pltpu.VMEM((D, NE), BF16)],
        compiler_params=_cp(("arbitrary", "arbitrary")), name="kv_latent",
    )(hn, w_ext, g_ckv.reshape(1, KV_LORA), cos_t, sin_t)


def _kv_heads_body(c_ref, pe_ref, wk_ref, wv_ref, gn_ref, gp_ref, k_ref, v_ref, wk_bf, wv_bf):
    @pl.when((pl.program_id(0) == 0) & (pl.program_id(1) == 0))
    def _():
        wk_bf[...] = wk_ref[...].astype(BF16)
        wv_bf[...] = wv_ref[...].astype(BF16)

    c = c_ref[...].astype(BF16)
    pe = pe_ref[...]
    sspe = jnp.sum(pe * pe, axis=-1, keepdims=True)
    for h in range(N_HEADS):
        kh = jnp.dot(c, wk_bf[:, h * QK_NOPE:(h + 1) * QK_NOPE], preferred_element_type=F32)
        ss = jnp.sum(kh * kh, axis=-1, keepdims=True) + sspe
        inv = lax.rsqrt(ss * (1.0 / QK_HEAD) + EPS)
        k_ref[h, :, 0:QK_NOPE] = (kh * inv * gn_ref[...]).astype(k_ref.dtype)
        k_ref[h, :, QK_NOPE:QK_HEAD] = (pe * inv * gp_ref[...]).astype(k_ref.dtype)
        v_ref[h] = jnp.dot(c, wv_bf[:, h * V_HEAD:(h + 1) * V_HEAD],
                           preferred_element_type=F32).astype(v_ref.dtype)


def kv_heads(ckv, kpe, w_uk, w_uv, g_k, *, tm=512):
    G, T, L = ckv.shape
    tm = _tile(T, tm)
    HN = N_HEADS * QK_NOPE
    HV = N_HEADS * V_HEAD
    return pl.pallas_call(
        _kv_heads_body, grid=(G, T // tm),
        in_specs=[pl.BlockSpec((None, tm, L), lambda g, i: (g, i, 0)),
                  pl.BlockSpec((None, tm, QK_ROPE), lambda g, i: (g, i, 0)),
                  pl.BlockSpec((L, HN), lambda g, i: (0, 0)),
                  pl.BlockSpec((L, HV), lambda g, i: (0, 0)),
                  pl.BlockSpec((1, QK_NOPE), lambda g, i: (0, 0)),
                  pl.BlockSpec((1, QK_ROPE), lambda g, i: (0, 0))],
        out_specs=[pl.BlockSpec((None, N_HEADS, tm, QK_HEAD), lambda g, i: (g, 0, i, 0)),
                   pl.BlockSpec((None, N_HEADS, tm, V_HEAD), lambda g, i: (g, 0, i, 0))],
        out_shape=[jax.ShapeDtypeStruct((G, N_HEADS, T, QK_HEAD), BF16),
                   jax.ShapeDtypeStruct((G, N_HEADS, T, V_HEAD), BF16)],
        scratch_shapes=[pltpu.VMEM((L, HN), BF16), pltpu.VMEM((L, HV), BF16)],
        compiler_params=_cp(("arbitrary", "arbitrary")), name="kv_heads",
    )(ckv, kpe, w_uk.reshape(L, HN), w_uv.reshape(L, HV), g_k[:QK_NOPE].reshape(1, QK_NOPE),
      g_k[QK_NOPE:].reshape(1, QK_ROPE))


def _q_body(cq_ref, w_ref, cos_ref, sin_ref, gn_ref, gp_ref, q_ref, wbf):
    @pl.when((pl.program_id(0) == 0) & (pl.program_id(1) == 0))
    def _():
        wbf[...] = w_ref[...].astype(BF16)

    cq = cq_ref[...]
    cos = cos_ref[...]
    sin = sin_ref[...]
    hw = QK_NOPE + LANES
    for h in range(N_HEADS):
        a = jnp.dot(cq, wbf[:, h * hw:(h + 1) * hw], preferred_element_type=F32)
        nope = a[:, :QK_NOPE]
        rot = _rope128(a[:, QK_NOPE:], cos, sin)
        ss = jnp.sum(nope * nope, axis=-1, keepdims=True) + jnp.sum(rot * rot, axis=-1, keepdims=True)
        inv = lax.rsqrt(ss * (1.0 / QK_HEAD) + EPS) * SCALE
        q_ref[h, :, 0:QK_NOPE] = (nope * inv * gn_ref[...]).astype(q_ref.dtype)
        q_ref[h, :, QK_NOPE:QK_HEAD] = (rot * inv * gp_ref[...])[:, :QK_ROPE].astype(q_ref.dtype)


def q_heads(cq, w_ext, cos_t, sin_t, g_q, out_dtype, *, tm=512):
    G, T, L = cq.shape
    tm = _tile(T, tm)
    NE = w_ext.shape[-1]
    if cos_t.shape[0] == 1:
        tab = pl.BlockSpec((1, LANES), lambda g, i: (0, 0))
    else:
        tab = pl.BlockSpec((tm, LANES), lambda g, i: (i, 0))
    gp = jnp.zeros((1, LANES), F32).at[0, :QK_ROPE].set(g_q[QK_NOPE:])
    return pl.pallas_call(
        _q_body, grid=(G, T // tm),
        in_specs=[pl.BlockSpec((None, tm, L), lambda g, i: (g, i, 0)),
                  pl.BlockSpec((L, NE), lambda g, i: (0, 0)), tab, tab,
                  pl.BlockSpec((1, QK_NOPE), lambda g, i: (0, 0)),
                  pl.BlockSpec((1, LANES), lambda g, i: (0, 0))],
        out_specs=pl.BlockSpec((None, N_HEADS, tm, QK_HEAD), lambda g, i: (g, 0, i, 0)),
        out_shape=jax.ShapeDtypeStruct((G, N_HEADS, T, QK_HEAD), out_dtype),
        scratch_shapes=[pltpu.VMEM((L, NE), BF16)],
        compiler_params=_cp(("arbitrary", "arbitrary")), name="q_heads",
    )(cq, w_ext, cos_t, sin_t, g_q[:QK_NOPE].reshape(1, QK_NOPE), gp)


def _flash_body(q_ref, k_ref, v_ref, o_ref, *, tb):
    i = pl.program_id(2)
    qa = q_ref[0:tb, :]
    qb = q_ref[tb:2 * tb, :]

    def kv_block(j):
        r0 = pl.multiple_of(j * tb, tb)
        return k_ref[pl.ds(r0, tb), :], v_ref[pl.ds(r0, tb), :]

    def scores(q, k, masked):
        s = lax.dot_general(q, k, NT_DIMS, preferred_element_type=F32)
        if masked:
            row = lax.broadcasted_iota(jnp.int32, (tb, tb), 0)
            col = lax.broadcasted_iota(jnp.int32, (tb, tb), 1)
            s = jnp.where(col <= row, s, -jnp.inf)
        return s

    def update(s, v, carry):
        m, l, acc = carry
        m_new = jnp.maximum(m, jnp.max(s, axis=-1, keepdims=True))
        p = jnp.exp(s - m_new)
        corr = jnp.exp(m - m_new)
        l = l * corr + jnp.sum(p, axis=-1, keepdims=True)
        acc = acc * corr + jnp.dot(p.astype(BF16), v, preferred_element_type=F32)
        return m_new, l, acc

    def both(j, carries, masked_a):
        k, v = kv_block(j)
        sa = scores(qa, k, masked_a)
        sb = scores(qb, k, False)
        return update(sa, v, carries[0]), update(sb, v, carries[1])

    init = (jnp.full((tb, 1), -jnp.inf, F32), jnp.zeros((tb, 1), F32), jnp.zeros((tb, V_HEAD), F32))
    carries = lax.fori_loop(0, 2 * i, lambda j, c: both(j, c, False), (init, init))
    ca, cb = both(2 * i, carries, True)
    k, v = kv_block(2 * i + 1)
    cb = update(scores(qb, k, True), v, cb)
    o_ref[0:tb, :] = (ca[2] / ca[1]).astype(o_ref.dtype)
    o_ref[tb:2 * tb, :] = (cb[2] / cb[1]).astype(o_ref.dtype)


def flash_prompt(q, k, v, *, tb=512):
    G, H, T, E = q.shape
    tb = tb if T % (2 * tb) == 0 else T // 2
    tq = 2 * tb
    return pl.pallas_call(
        functools.partial(_flash_body, tb=tb),
        grid=(G, H, T // tq),
        in_specs=[pl.BlockSpec((None, None, tq, E), lambda g, h, i: (g, h, i, 0)),
                  pl.BlockSpec((None, None, T, E), lambda g, h, i: (g, h, 0, 0)),
                  pl.BlockSpec((None, None, T, V_HEAD), lambda g, h, i: (g, h, 0, 0))],
        out_specs=pl.BlockSpec((None, tq, V_HEAD), lambda g, h, i: (g, i, h)),
        out_shape=jax.ShapeDtypeStruct((G, T, H * V_HEAD), BF16),
        compiler_params=_cp(("arbitrary", "arbitrary", "arbitrary")), name="flash_prompt",
    )(q, k, v)


def _absorb_body(q_ref, k_ref, w_ref, gn_ref, gp_ref, qlat_ref, qpe_ref, sn_ref):
    q = q_ref[...]
    qn = (q[:, :QK_NOPE] * gn_ref[...]).astype(BF16)
    qlat_ref[...] = lax.dot_general(qn, w_ref[...].astype(BF16), NT_DIMS,
                                    preferred_element_type=F32).astype(qlat_ref.dtype)
    qpe_ref[...] = (q[:, QK_NOPE:] * gp_ref[...]).astype(qpe_ref.dtype)
    s = jnp.sum(q.astype(BF16).astype(F32) * k_ref[...].astype(F32), axis=-1, keepdims=True)
    sn_ref[...] = jnp.broadcast_to(s, sn_ref.shape)


def absorb_queries(q, k, w_uk, g_k):
    _, H, B, E = q.shape
    L = w_uk.shape[0]
    return pl.pallas_call(
        _absorb_body, grid=(H,),
        in_specs=[pl.BlockSpec((None, None, B, E), lambda h: (0, h, 0, 0)),
                  pl.BlockSpec((None, None, B, E), lambda h: (0, h, 0, 0)),
                  pl.BlockSpec((L, QK_NOPE), lambda h: (0, h)),
                  pl.BlockSpec((1, QK_NOPE), lambda h: (0, 0)),
                  pl.BlockSpec((1, QK_ROPE), lambda h: (0, 0))],
        out_specs=[pl.BlockSpec((None, B, L), lambda h: (h, 0, 0)),
                   pl.BlockSpec((None, B, QK_ROPE), lambda h: (h, 0, 0)),
                   pl.BlockSpec((None, B, LANES), lambda h: (h, 0, 0))],
        out_shape=[jax.ShapeDtypeStruct((H, B, L), BF16), jax.ShapeDtypeStruct((H, B, QK_ROPE), BF16),
                   jax.ShapeDtypeStruct((H, B, LANES), F32)],
        compiler_params=_cp(("arbitrary",)), name="absorb_queries",
    )(q, k, w_uk.reshape(L, H * QK_NOPE), g_k[:QK_NOPE].reshape(1, QK_NOPE),
      g_k[QK_NOPE:].reshape(1, QK_ROPE))


PAGES_PER_HALF = 16


def _page_copies(pt_ref, ck_hbm, kp_hbm, cbuf, kbuf, sem_c, sem_k, n, first, pp):
    copies = []
    for k in range(pp):
        page = pt_ref[n, first + k]
        copies.append(pltpu.make_async_copy(ck_hbm.at[page], cbuf.at[k], sem_c))
        copies.append(pltpu.make_async_copy(kp_hbm.at[page], kbuf.at[k], sem_k))
    return copies


def _paged_scores(waug, qpe, cbuf, kbuf):
    pp, pg, L = cbuf.shape
    HN = N_HEADS * QK_NOPE
    P = pp * pg
    cc = cbuf[...].reshape(P, L).astype(BF16)
    kpt = jnp.concatenate([kbuf[k] for k in range(pp)], axis=1)
    kt = lax.dot_general(waug[...], cc, NT_DIMS, preferred_element_type=F32)
    kn = kt[0:HN, :].reshape(N_HEADS, QK_NOPE, P)
    ssn = jnp.sum(kn * kn, axis=1)
    s_lat = kt[HN:HN + N_HEADS, :]
    sspe = jnp.sum(kpt * kpt, axis=0, keepdims=True)
    s_pe = jnp.dot(qpe, kpt.astype(BF16), preferred_element_type=F32)
    inv = lax.rsqrt((ssn + sspe) * (1.0 / QK_HEAD) + EPS)
    return (s_lat + s_pe) * inv, cc


def _paged_accumulate(s, cc, m_old, l_old, acc_old):
    m_new = jnp.maximum(m_old, jnp.max(s, axis=-1, keepdims=True))
    pexp = jnp.exp(s - m_new[:, 0:1])
    corr = jnp.exp(m_old - m_new)
    l_new = l_old * corr + jnp.sum(pexp, axis=-1, keepdims=True)
    acc_new = acc_old * corr[:, 0:1] + jnp.dot(pexp.astype(BF16), cc, preferred_element_type=F32)
    return m_new, l_new, acc_new


def _paged_body(pt_ref, qlat_ref, qpe_ref, sn_ref, cnew_ref, wt_ref, ck_hbm, kp_hbm, o_ref,
                waug, m_s, l_s, acc_s, cbuf_a, kbuf_a, cbuf_b, kbuf_b, sems, *, pp):
    n = pl.program_id(0)
    p = pl.program_id(1)
    steps = pl.num_programs(1)
    t = n * steps + p
    last_t = pl.num_programs(0) * steps - 1
    HN = N_HEADS * QK_NOPE

    def copies_a(nn, ps):
        return _page_copies(pt_ref, ck_hbm, kp_hbm, cbuf_a, kbuf_a, sems.at[0], sems.at[1], nn, 2 * ps * pp, pp)

    def copies_b(nn, ps):
        return _page_copies(pt_ref, ck_hbm, kp_hbm, cbuf_b, kbuf_b, sems.at[2], sems.at[3], nn,
                            (2 * ps + 1) * pp, pp)

    @pl.when(t == 0)
    def _():
        waug[0:HN, :] = wt_ref[...]
        for c in copies_a(n, p):
            c.start()

    for c in copies_b(n, p):
        c.start()

    @pl.when(p == 0)
    def _():
        waug[HN:HN + N_HEADS, :] = qlat_ref[...]
        m_s[...] = sn_ref[...]
        l_s[...] = jnp.ones(l_s.shape, F32)
        acc_s[...] = jnp.broadcast_to(cnew_ref[...], acc_s.shape)

    qpe = qpe_ref[...]
    state = (m_s[...], l_s[...], acc_s[...])
    for c in copies_a(n, p):
        c.wait()
    s_a, cc_a = _paged_scores(waug, qpe, cbuf_a, kbuf_a)

    nxt = jnp.minimum(t + 1, last_t)
    n_nxt = nxt // steps
    p_nxt = nxt - n_nxt * steps
    for c in copies_a(n_nxt, p_nxt):
        c.start()

    for c in copies_b(n, p):
        c.wait()
    s_b, cc_b = _paged_scores(waug, qpe, cbuf_b, kbuf_b)
    state = _paged_accumulate(s_a, cc_a, *state)
    m, l, acc = _paged_accumulate(s_b, cc_b, *state)
    m_s[...] = m
    l_s[...] = l
    acc_s[...] = acc

    @pl.when(t == last_t)
    def _():
        for c in copies_a(n_nxt, p_nxt):
            c.wait()

    @pl.when(p == steps - 1)
    def _():
        o_ref[...] = acc / l[:, 0:1]


def paged_attention(page_table, cache_ckv, cache_kpe_t, qlat, qpe, sn, c_new, w_uk_t):
    B, NP = page_table.shape
    L = cache_ckv.shape[-1]
    pp = next(c for c in (PAGES_PER_HALF, 8, 4, 2, 1) if c <= PAGES_PER_HALF and NP % (2 * c) == 0)
    HN = N_HEADS * QK_NOPE
    per_seq = lambda w: pl.BlockSpec((None, N_HEADS, w), lambda n, p, pt: (n, 0, 0))
    grid_spec = pltpu.PrefetchScalarGridSpec(
        num_scalar_prefetch=1,
        grid=(B, NP // (2 * pp)),
        in_specs=[per_seq(L), per_seq(QK_ROPE), per_seq(LANES),
                  pl.BlockSpec((None, 1, L), lambda n, p, pt: (n, 0, 0)),
                  pl.BlockSpec((HN, L), lambda n, p, pt: (0, 0)),
                  pl.BlockSpec(memory_space=pl.ANY), pl.BlockSpec(memory_space=pl.ANY)],
        out_specs=per_seq(L),
        scratch_shapes=[pltpu.VMEM((HN + N_HEADS, L), BF16), pltpu.VMEM((N_HEADS, LANES), F32),
                        pltpu.VMEM((N_HEADS, LANES), F32), pltpu.VMEM((N_HEADS, L), F32),
                        pltpu.VMEM((pp, PAGE, L), F32), pltpu.VMEM((pp, QK_ROPE, PAGE), F32),
                        pltpu.VMEM((pp, PAGE, L), F32), pltpu.VMEM((pp, QK_ROPE, PAGE), F32),
                        pltpu.SemaphoreType.DMA((4,))],
    )
    return pl.pallas_call(
        functools.partial(_paged_body, pp=pp),
        grid_spec=grid_spec,
        out_shape=jax.ShapeDtypeStruct((B, N_HEADS, L), F32),
        compiler_params=_cp(("arbitrary", "arbitrary")), name="paged_attention",
    )(page_table, qlat, qpe, sn, c_new, w_uk_t, cache_ckv, cache_kpe_t)


def _head_proj_body(x_ref, w_ref, o_ref):
    o_ref[...] = jnp.dot(x_ref[...].astype(BF16), w_ref[...].astype(BF16),
                         preferred_element_type=F32).astype(o_ref.dtype)


def head_value_proj(o_lat, w_uv):
    H, B, L = o_lat.shape
    return pl.pallas_call(
        _head_proj_body, grid=(H,),
        in_specs=[pl.BlockSpec((None, B, L), lambda h: (h, 0, 0)),
                  pl.BlockSpec((L, V_HEAD), lambda h: (0, h))],
        out_specs=pl.BlockSpec((None, B, V_HEAD), lambda h: (0, 0, h)),
        out_shape=jax.ShapeDtypeStruct((1, B, H * V_HEAD), BF16),
        compiler_params=_cp(("arbitrary",)), name="head_value_proj",
    )(o_lat, w_uv)


def _new_expert(te_ref, r):
    prev = te_ref[jnp.maximum(r - 1, 0)]
    return (r == 0) | (te_ref[r] != prev)


def _moe_up_body(te_ref, tv_ref, x_ref, wg_ref, wu_ref, o_ref, wg_bf, wu_bf):
    r = pl.program_id(1)

    @pl.when(_new_expert(te_ref, r))
    def _():
        wg_bf[...] = wg_ref[...].astype(BF16)
        wu_bf[...] = wu_ref[...].astype(BF16)

    @pl.when(tv_ref[r] > 0)
    def _():
        x = x_ref[...].astype(BF16)
        a = jnp.dot(x, wg_bf[...], preferred_element_type=F32)
        u = jnp.dot(x, wu_bf[...], preferred_element_type=F32)
        o_ref[...] = (a * jax.nn.sigmoid(a) * u).astype(o_ref.dtype)

    @pl.when(tv_ref[r] == 0)
    def _():
        o_ref[...] = jnp.zeros(o_ref.shape, o_ref.dtype)


def _moe_down_body(te_ref, tv_ref, h_ref, wd_ref, o_ref, wd_bf):
    r = pl.program_id(1)

    @pl.when(_new_expert(te_ref, r))
    def _():
        wd_bf[...] = wd_ref[...].astype(BF16)

    @pl.when(tv_ref[r] > 0)
    def _():
        o_ref[...] = jnp.dot(h_ref[...], wd_bf[...], preferred_element_type=F32)

    @pl.when(tv_ref[r] == 0)
    def _():
        o_ref[...] = jnp.zeros(o_ref.shape, o_ref.dtype)


def moe_experts(xg, tile_expert, tile_valid, w_gate, w_up, w_down, *, tm, tf=512, tn=512):
    R, D = xg.shape
    F = w_gate.shape[-1]
    tf = _tile(F, tf)
    tn = _tile(D, tn)
    up_spec = pltpu.PrefetchScalarGridSpec(
        num_scalar_prefetch=2, grid=(F // tf, R // tm),
        in_specs=[pl.BlockSpec((tm, D), lambda f, r, te, tv: (r, 0)),
                  pl.BlockSpec((None, D, tf), lambda f, r, te, tv: (te[r], 0, f)),
                  pl.BlockSpec((None, D, tf), lambda f, r, te, tv: (te[r], 0, f))],
        out_specs=pl.BlockSpec((tm, tf), lambda f, r, te, tv: (r, f)),
        scratch_shapes=[pltpu.VMEM((D, tf), BF16), pltpu.VMEM((D, tf), BF16)])
    h = pl.pallas_call(
        _moe_up_body, grid_spec=up_spec, out_shape=jax.ShapeDtypeStruct((R, F), BF16),
        compiler_params=_cp(("arbitrary", "arbitrary")), name="moe_up",
    )(tile_expert, tile_valid, xg, w_gate, w_up)
    down_spec = pltpu.PrefetchScalarGridSpec(
        num_scalar_prefetch=2, grid=(D // tn, R // tm),
        in_specs=[pl.BlockSpec((tm, F), lambda j, r, te, tv: (r, 0)),
                  pl.BlockSpec((None, F, tn), lambda j, r, te, tv: (te[r], 0, j))],
        out_specs=pl.BlockSpec((tm, tn), lambda j, r, te, tv: (r, j)),
        scratch_shapes=[pltpu.VMEM((F, tn), BF16)])
    return pl.pallas_call(
        _moe_down_body, grid_spec=down_spec, out_shape=jax.ShapeDtypeStruct((R, D), F32),
        compiler_params=_cp(("arbitrary", "arbitrary")), name="moe_down",
    )(tile_expert, tile_valid, h, w_down)


def _combine_body(x_ref, g_ref, ya_ref, yb_ref, w_ref, o_ref):
    w = w_ref[...]
    f = w[:, 0:1] * ya_ref[...] + w[:, 1:2] * yb_ref[...]
    o_ref[...] = x_ref[...] + g_ref[...] * f


def moe_combine(x, gate_mod, gate_chunk, ya, yb, w, *, tm=256):
    G, T, D = x.shape
    R = gate_mod.shape[1]
    tm = _tile(T, tm)
    tspec = pl.BlockSpec((None, tm, D), lambda g, i: (g, i, 0))
    return pl.pallas_call(
        _combine_body, grid=(G, T // tm),
        in_specs=[tspec, pl.BlockSpec((None, R, D), lambda g, i: (g, 0, gate_chunk)), tspec, tspec,
                  pl.BlockSpec((None, tm, LANES), lambda g, i: (g, i, 0))],
        out_specs=tspec, out_shape=jax.ShapeDtypeStruct((G, T, D), F32),
        compiler_params=_cp(("arbitrary", "arbitrary")), name="moe_combine",
    )(x, gate_mod, ya, yb, w)


def _moe_layout(ids, tm):
    A = ids.shape[0]
    R = -(-(A + N_EXPERTS * (tm - 1)) // tm) * tm
    onehot = (ids[:, None] == jnp.arange(N_EXPERTS, dtype=jnp.int32)[None, :]).astype(jnp.int32)
    rank = jnp.cumsum(onehot, axis=0) - onehot
    counts = jnp.sum(onehot, axis=0)
    tiles = (counts + tm - 1) // tm
    tile_end = jnp.cumsum(tiles)
    row_off = (tile_end - tiles) * tm
    pos = jnp.sum(onehot * (row_off[None, :] + rank), axis=1)
    tile_id = jnp.arange(R // tm, dtype=jnp.int32)
    tile_expert = jnp.minimum(jnp.sum((tile_id[:, None] >= tile_end[None, :]).astype(jnp.int32), axis=1),
                              N_EXPERTS - 1).astype(jnp.int32)
    tile_valid = (tile_id < tile_end[-1]).astype(jnp.int32)
    return R, pos.astype(jnp.int32), tile_expert, tile_valid


def _rope_tables(pos):
    half = QK_ROPE // 2
    inv = ROPE_THETA ** (-jnp.arange(half, dtype=F32) / half)
    ang = pos.astype(F32)[:, None] * inv[None, :]
    cos, sin = jnp.cos(ang), jnp.sin(ang)
    z = jnp.zeros((pos.shape[0], LANES - QK_ROPE), F32)
    return jnp.concatenate([cos, cos, z], axis=-1), jnp.concatenate([-sin, sin, z], axis=-1)


def _swap_halves(w):
    half = QK_ROPE // 2
    return jnp.concatenate([w[..., half:], w[..., :half]], axis=-1)


def kernel(x_prompt, x_sample, state_rglru_h, state_rglru_conv, cache_ckv, cache_kpe, page_table, c_prompt, c_sample, w_ada, b_ada, g_norm, rg_w_y, rg_b_y, rg_w_x, rg_b_x, rg_conv_w, rg_conv_b, rg_w_a, rg_b_a, rg_w_i, rg_b_i, rg_lam, rg_w_out, rg_b_out, kv_w_ada, kv_b_ada, kv_g_in, kv_w_dkv, kv_g_ckv, kv_w_uk, kv_w_uv, kv_g_k, mla_w_dq, mla_g_cq, mla_w_uq, mla_g_q, mla_w_o, ffn_w_gate, ffn_w_up, ffn_w_down, moe_w_router, moe_w_gate, moe_w_up, moe_w_down):
    Bp, S, D = x_prompt.shape
    Bs = x_sample.shape[0]
    past = page_table.shape[1] * PAGE
    HN = N_HEADS * QK_NOPE
    HV = N_HEADS * V_HEAD

    xs = {"p": x_prompt, "s": x_sample.reshape(1, Bs, D)}
    groups = ("p", "s")

    pad = (-(Bs + Bp)) % 16
    c_all = jnp.concatenate([c_sample, c_prompt, jnp.zeros((pad, D), F32)], axis=0)[None]

    def split_mods(m):
        return {"p": m[0, Bs:Bs + Bp].reshape(Bp, 1, -1), "s": m[:, :Bs]}

    mods = [split_mods(mm(c_all, w_ada, (l,), bias=b_ada[l], pre_silu=True, name="adaln"))
            for l in range(2)]
    mods_kv = split_mods(mm(c_all, kv_w_ada, bias=kv_b_ada, pre_silu=True, name="adaln_kv"))

    x1 = {}
    h_last = {}
    conv_last = {}
    for gname in groups:
        x = xs[gname]
        md = mods[0][gname]
        xn = norm_mod(x, g_norm[0, 0], md, 0, 1)
        y, xr = gelu_and_linear(xn, rg_w_y, rg_w_x, (0,), rg_b_y[0], rg_b_x[0])
        if gname == "p":
            hy, hl = rglru_prompt(xr, y, rg_conv_w[0], rg_conv_b[0], rg_w_a[0], rg_w_i[0], rg_b_a[0],
                                  rg_b_i[0], rg_lam[0])
            h_last[gname] = hl.reshape(1, Bp, D)
            conv_last[gname] = xr[:, S - (CONV_W - 1):, :][None]
        else:
            hy, hl = rglru_sample(xr, y, state_rglru_conv[0], state_rglru_h[0], rg_conv_w[0], rg_conv_b[0],
                                  rg_w_a[0], rg_w_i[0], rg_b_a[0], rg_b_i[0], rg_lam[0])
            h_last[gname] = hl[None]
            conv_last[gname] = jnp.concatenate([state_rglru_conv[0][:, 1:], xr.reshape(Bs, 1, D)], axis=1)[None]
        xa = mm(hy, rg_w_out, (0,), bias=rg_b_out[0], res=x, gate=md, gate_chunk=2, tn=1024, name="rg_out")
        xn = norm_mod(xa, g_norm[0, 1], md, 3, 4)
        hmid = swiglu_up(xn, ffn_w_gate, ffn_w_up, (0,))
        x1[gname] = mm(hmid, ffn_w_down, (0,), res=xa, gate=md, gate_chunk=5, name="ffn_down")

    w_dkv_ext = jnp.concatenate([kv_w_dkv, _swap_halves(kv_w_dkv[:, KV_LORA:])], axis=-1)
    w_uq = mla_w_uq[0]
    w_uq_ext = jnp.concatenate([w_uq, _swap_halves(w_uq[..., QK_NOPE:])], axis=-1).reshape(
        w_uq.shape[0], N_HEADS * (QK_NOPE + LANES))
    tabs = {"p": _rope_tables(jnp.arange(S)), "s": _rope_tables(jnp.full((1,), past))}
    w_uk2 = kv_w_uk.reshape(KV_LORA, HN)
    w_uv2 = kv_w_uv.reshape(KV_LORA, HV)

    ckv = {}
    kpe = {}
    xb = {}
    for gname in groups:
        x = x1[gname]
        md = mods[1][gname]
        cos_t, sin_t = tabs[gname]
        hn = norm_mod(x, kv_g_in, mods_kv[gname], 0, 1)
        ckv[gname], kpe[gname] = kv_latent(hn, w_dkv_ext, kv_g_ckv, cos_t, sin_t)
        k, v = kv_heads(ckv[gname], kpe[gname], w_uk2, w_uv2, kv_g_k)
        xn = norm_mod(x, g_norm[1, 0], md, 0, 1)
        cq = mm(xn, mla_w_dq, (0,), rms_g=mla_g_cq[0], out_dtype=BF16, name="q_down")
        if gname == "p":
            q = q_heads(cq, w_uq_ext, cos_t, sin_t, mla_g_q[0], BF16)
            o = flash_prompt(q, k, v)
        else:
            q = q_heads(cq, w_uq_ext, cos_t, sin_t, mla_g_q[0], F32)
            qlat, qpe, sn = absorb_queries(q, k, w_uk2, kv_g_k)
            o_lat = paged_attention(page_table, cache_ckv, jnp.swapaxes(cache_kpe, 1, 2), jnp.swapaxes(qlat, 0, 1),
                                    jnp.swapaxes(qpe, 0, 1), jnp.swapaxes(sn, 0, 1),
                                    ckv[gname].reshape(Bs, 1, KV_LORA), w_uk2.T.astype(BF16))
            o = head_value_proj(jnp.swapaxes(o_lat, 0, 1), w_uv2)
        xb[gname] = mm(o, mla_w_o, (0,), res=x, gate=md, gate_chunk=2, tn=1024, name="attn_out")

    xn_l, gate_l, idx_l = [], [], []
    for gname in groups:
        xn, gt, ix = norm_mod(xb[gname], g_norm[1, 1], mods[1][gname], 3, 4, router_w=moe_w_router[0])
        xn_l.append(xn.reshape(-1, D))
        gate_l.append(gt)
        idx_l.append(ix[..., :2].reshape(-1, 2))
    xn_all = jnp.concatenate(xn_l, axis=0)
    ids = jnp.concatenate(idx_l, axis=0).reshape(-1)
    tm_moe = 512
    R, pos, tile_expert, tile_valid = _moe_layout(ids, tm_moe)
    tok_of_row = jnp.zeros((R,), jnp.int32).at[pos].set(jnp.arange(ids.shape[0], dtype=jnp.int32) // 2)
    xg = xn_all.at[tok_of_row].get(mode="promise_in_bounds")
    yg = moe_experts(xg, tile_expert, tile_valid, moe_w_gate[0], moe_w_up[0], moe_w_down[0], tm=tm_moe)
    pos2 = pos.reshape(-1, 2)
    n_p = Bp * S
    out = {}
    for gi, gname in enumerate(groups):
        sl = slice(0, n_p) if gname == "p" else slice(n_p, n_p + Bs)
        shp = xb[gname].shape
        ya = yg.at[pos2[sl, 0]].get(mode="promise_in_bounds").reshape(shp)
        yb = yg.at[pos2[sl, 1]].get(mode="promise_in_bounds").reshape(shp)
        out[gname] = moe_combine(xb[gname], mods[1][gname], 5, ya, yb, gate_l[gi])

    return (out["p"], out["s"].reshape(Bs, 1, D),
            h_last["p"], conv_last["p"], ckv["p"], kpe["p"],
            h_last["s"], conv_last["s"], ckv["s"].reshape(Bs, 1, KV_LORA), kpe["s"].reshape(Bs, 1, QK_ROPE))
```

```python
import functools
import math

import jax
import jax.numpy as jnp
from jax import lax
from jax.experimental import pallas as pl
from jax.experimental.pallas import tpu as pltpu

F32 = jnp.float32
BF16 = jnp.bfloat16

N_GATE_BLOCKS = 8
CONV_W = 4
RG_C = 8.0
N_HEADS = 16
KV_LORA = 512
QK_NOPE = 128
QK_ROPE = 64
QK_HEAD = QK_NOPE + QK_ROPE
V_HEAD = 128
ROPE_THETA = 10000.0
SCALE = QK_HEAD ** -0.5
N_EXPERTS = 8
EPS = 1e-6
PAGE = 128
LANES = 128
SUBLANES = 8
VMEM_LIMIT = 56 * 1024 * 1024
NT_DIMS = (((1,), (1,)), ((), ()))


def _cp(sem):
    return pltpu.CompilerParams(dimension_semantics=sem, vmem_limit_bytes=VMEM_LIMIT)


def _tile(n, pref):
    return pref if n % pref == 0 else n


def _mm_body(*refs, act, has_bias, has_res, has_rms, pre_silu):
    x_ref, w_ref = refs[0], refs[1]
    k = 2
    b_ref = res_ref = gate_ref = g_ref = None
    if has_bias:
        b_ref = refs[k]; k += 1
    if has_res:
        res_ref, gate_ref = refs[k], refs[k + 1]; k += 2
    if has_rms:
        g_ref = refs[k]; k += 1
    o_ref, wbf = refs[k], refs[k + 1]

    @pl.when((pl.program_id(1) == 0) & (pl.program_id(2) == 0))
    def _():
        wbf[...] = w_ref[...].astype(BF16)

    x = x_ref[...]
    if pre_silu:
        x = x * jax.nn.sigmoid(x)
    acc = jnp.dot(x.astype(BF16), wbf[...], preferred_element_type=F32)
    if has_bias:
        acc = acc + b_ref[...]
    if act == "gelu":
        acc = jax.nn.gelu(acc)
    if has_rms:
        acc = acc * lax.rsqrt(jnp.mean(acc * acc, axis=-1, keepdims=True) + EPS) * g_ref[...]
    if has_res:
        acc = res_ref[...] + gate_ref[...] * acc
    o_ref[...] = acc.astype(o_ref.dtype)


def mm(x, w, widx=(), *, bias=None, act=None, res=None, gate=None, gate_chunk=0,
       rms_g=None, pre_silu=False, out_dtype=F32, tm=512, tn=512, name="mm"):
    G, T, K = x.shape
    N = w.shape[-1]
    tm = _tile(T, tm)
    tn = _tile(N, tn)
    nb = N // tn
    widx = tuple(widx)
    in_specs = [
        pl.BlockSpec((None, tm, K), lambda j, g, i: (g, i, 0)),
        pl.BlockSpec((None,) * len(widx) + (K, tn), lambda j, g, i: widx + (0, j)),
    ]
    args = [x, w]
    if bias is not None:
        in_specs.append(pl.BlockSpec((1, tn), lambda j, g, i: (0, j)))
        args.append(bias.reshape(1, N))
    if res is not None:
        R = gate.shape[1]
        in_specs.append(pl.BlockSpec((None, tm, tn), lambda j, g, i: (g, i, j)))
        in_specs.append(pl.BlockSpec((None, R, tn), lambda j, g, i: (g, 0, gate_chunk * nb + j)))
        args += [res, gate]
    if rms_g is not None:
        assert tn == N
        in_specs.append(pl.BlockSpec((1, tn), lambda j, g, i: (0, j)))
        args.append(rms_g.reshape(1, N))
    body = functools.partial(_mm_body, act=act, has_bias=bias is not None, has_res=res is not None,
                             has_rms=rms_g is not None, pre_silu=pre_silu)
    return pl.pallas_call(
        body,
        grid=(nb, G, T // tm),
        in_specs=in_specs,
        out_specs=pl.BlockSpec((None, tm, tn), lambda j, g, i: (g, i, j)),
        out_shape=jax.ShapeDtypeStruct((G, T, N), out_dtype),
        scratch_shapes=[pltpu.VMEM((K, tn), BF16)],
        compiler_params=_cp(("arbitrary", "arbitrary", "arbitrary")),
        name=name,
    )(*args)


def _swiglu_body(x_ref, wg_ref, wu_ref, o_ref, wg_bf, wu_bf):
    @pl.when((pl.program_id(1) == 0) & (pl.program_id(2) == 0))
    def _():
        wg_bf[...] = wg_ref[...].astype(BF16)
        wu_bf[...] = wu_ref[...].astype(BF16)

    x = x_ref[...]
    a = jnp.dot(x, wg_bf[...], preferred_element_type=F32)
    u = jnp.dot(x, wu_bf[...], preferred_element_type=F32)
    o_ref[...] = (a * jax.nn.sigmoid(a) * u).astype(o_ref.dtype)


def swiglu_up(x, wg, wu, widx, *, tm=512, tn=512):
    G, T, K = x.shape
    N = wg.shape[-1]
    tm = _tile(T, tm)
    tn = _tile(N, tn)
    widx = tuple(widx)
    wspec = pl.BlockSpec((None,) * len(widx) + (K, tn), lambda j, g, i: widx + (0, j))
    return pl.pallas_call(
        _swiglu_body,
        grid=(N // tn, G, T // tm),
        in_specs=[pl.BlockSpec((None, tm, K), lambda j, g, i: (g, i, 0)), wspec, wspec],
        out_specs=pl.BlockSpec((None, tm, tn), lambda j, g, i: (g, i, j)),
        out_shape=jax.ShapeDtypeStruct((G, T, N), BF16),
        scratch_shapes=[pltpu.VMEM((K, tn), BF16), pltpu.VMEM((K, tn), BF16)],
        compiler_params=_cp(("arbitrary", "arbitrary", "arbitrary")),
        name="swiglu_up",
    )(x, wg, wu)


def _dual_body(x_ref, w1_ref, w2_ref, b1_ref, b2_ref, o1_ref, o2_ref, w1_bf, w2_bf):
    @pl.when((pl.program_id(1) == 0) & (pl.program_id(2) == 0))
    def _():
        w1_bf[...] = w1_ref[...].astype(BF16)
        w2_bf[...] = w2_ref[...].astype(BF16)

    x = x_ref[...]
    o1_ref[...] = jax.nn.gelu(jnp.dot(x, w1_bf[...], preferred_element_type=F32) + b1_ref[...])
    o2_ref[...] = jnp.dot(x, w2_bf[...], preferred_element_type=F32) + b2_ref[...]


def gelu_and_linear(x, w1, w2, widx, b1, b2, *, tm=512, tn=512):
    G, T, K = x.shape
    N = w1.shape[-1]
    tm = _tile(T, tm)
    tn = _tile(N, tn)
    widx = tuple(widx)
    wspec = pl.BlockSpec((None,) * len(widx) + (K, tn), lambda j, g, i: widx + (0, j))
    bspec = pl.BlockSpec((1, tn), lambda j, g, i: (0, j))
    ospec = pl.BlockSpec((None, tm, tn), lambda j, g, i: (g, i, j))
    return pl.pallas_call(
        _dual_body,
        grid=(N // tn, G, T // tm),
        in_specs=[pl.BlockSpec((None, tm, K), lambda j, g, i: (g, i, 0)), wspec, wspec, bspec, bspec],
        out_specs=[ospec, ospec],
        out_shape=[jax.ShapeDtypeStruct((G, T, N), F32), jax.ShapeDtypeStruct((G, T, N), F32)],
        scratch_shapes=[pltpu.VMEM((K, tn), BF16), pltpu.VMEM((K, tn), BF16)],
        compiler_params=_cp(("arbitrary", "arbitrary", "arbitrary")),
        name="rg_in_proj",
    )(x, w1, w2, b1.reshape(1, N), b2.reshape(1, N))


def _modulated_norm(x, g, shift, scale):
    y = x * lax.rsqrt(jnp.mean(x * x, axis=-1, keepdims=True) + EPS)
    return (y * g) * (1.0 + scale) + shift


def _norm_body(x_ref, g_ref, sh_ref, sc_ref, o_ref):
    o_ref[...] = _modulated_norm(x_ref[...], g_ref[...], sh_ref[...], sc_ref[...]).astype(o_ref.dtype)


def _norm_router_body(x_ref, g_ref, sh_ref, sc_ref, wr_ref, o_ref, gate_ref, idx_ref):
    xn = _modulated_norm(x_ref[...], g_ref[...], sh_ref[...], sc_ref[...])
    o_ref[...] = xn.astype(o_ref.dtype)
    logits = jnp.dot(xn, wr_ref[...], preferred_element_type=F32, precision=lax.Precision.HIGHEST)
    lane_i = lax.broadcasted_iota(jnp.int32, logits.shape, 1)
    lane = lane_i.astype(F32)
    neg = jnp.float32(-jnp.inf)
    lg = jnp.where(lane_i < N_EXPERTS, logits, neg)
    m1 = jnp.max(lg, axis=-1, keepdims=True)
    i1 = jnp.min(jnp.where(lg == m1, lane, float(LANES)), axis=-1, keepdims=True)
    lg2 = jnp.where(lane == i1, neg, lg)
    m2 = jnp.max(lg2, axis=-1, keepdims=True)
    i2 = jnp.min(jnp.where(lg2 == m2, lane, float(LANES)), axis=-1, keepdims=True)
    e2 = jnp.exp(m2 - m1)
    g1 = 1.0 / (1.0 + e2)
    g2 = e2 * g1
    gate_ref[...] = jnp.where(lane_i == 0, g1, jnp.where(lane_i == 1, g2, 0.0))
    idx_ref[...] = jnp.where(lane_i == 0, i1, jnp.where(lane_i == 1, i2, 0.0)).astype(jnp.int32)


def norm_mod(x, gain, mods, k_shift, k_scale, *, router_w=None, tm=256):
    G, T, D = x.shape
    R = mods.shape[1]
    tm = _tile(T, tm)
    if R != 1:
        assert R == T and tm == T
    in_specs = [
        pl.BlockSpec((None, tm, D), lambda g, i: (g, i, 0)),
        pl.BlockSpec((1, D), lambda g, i: (0, 0)),
        pl.BlockSpec((None, R, D), lambda g, i: (g, 0, k_shift)),
        pl.BlockSpec((None, R, D), lambda g, i: (g, 0, k_scale)),
    ]
    args = [x, gain.reshape(1, D), mods, mods]
    xspec = pl.BlockSpec((None, tm, D), lambda g, i: (g, i, 0))
    if router_w is None:
        return pl.pallas_call(
            _norm_body, grid=(G, T // tm), in_specs=in_specs, out_specs=xspec,
            out_shape=jax.ShapeDtypeStruct((G, T, D), BF16),
            compiler_params=_cp(("arbitrary", "arbitrary")), name="norm_mod",
        )(*args)
    wr = jnp.zeros((D, LANES), F32).at[:, :N_EXPERTS].set(router_w)
    in_specs.append(pl.BlockSpec((D, LANES), lambda g, i: (0, 0)))
    lspec = pl.BlockSpec((None, tm, LANES), lambda g, i: (g, i, 0))
    return pl.pallas_call(
        _norm_router_body, grid=(G, T // tm), in_specs=in_specs,
        out_specs=[xspec, lspec, lspec],
        out_shape=[jax.ShapeDtypeStruct((G, T, D), BF16),
                   jax.ShapeDtypeStruct((G, T, LANES), F32),
                   jax.ShapeDtypeStruct((G, T, LANES), jnp.int32)],
        compiler_params=_cp(("arbitrary", "arbitrary")), name="norm_router",
    )(*args, wr)


def _rg_gates(xc, wa, wi, ba, bi, lam):
    xb = xc.astype(BF16)
    r = 0.5 * jnp.tanh(0.5 * (jnp.dot(xb, wa.astype(BF16), preferred_element_type=F32) + ba)) + 0.5
    gi = 0.5 * jnp.tanh(0.5 * (jnp.dot(xb, wi.astype(BF16), preferred_element_type=F32) + bi)) + 0.5
    neg_lam = -lam
    softplus = jnp.maximum(neg_lam, 0.0) + jnp.log1p(jnp.exp(-jnp.abs(neg_lam)))
    log_a = -RG_C * r * softplus
    a = jnp.exp(log_a)
    th = jnp.tanh(log_a)
    mult = jnp.sqrt(-2.0 * th / (1.0 - th))
    return a, mult, gi


def _rglru_prompt_body(xr_ref, y_ref, cw_ref, cb_ref, wa_ref, wi_ref, ba_ref, bi_ref, lam_ref,
                       hy_ref, hl_ref, ext, a_s, b_s, h_s, *, tc):
    t = pl.program_id(2)
    C = xr_ref.shape[-1]

    @pl.when(t == 0)
    def _():
        ext[0:SUBLANES, :] = jnp.zeros((SUBLANES, C), F32)
        h_s[...] = jnp.zeros((SUBLANES, C), F32)

    ext[SUBLANES:SUBLANES + tc, :] = xr_ref[...]
    xc = cb_ref[...]
    for k in range(CONV_W):
        off = SUBLANES - (CONV_W - 1) + k
        xc = xc + cw_ref[k:k + 1, :] * ext[off:off + tc, :]
    ext[0:SUBLANES, :] = ext[tc:tc + SUBLANES, :]

    a, mult, gi = _rg_gates(xc, wa_ref[...], wi_ref[...], ba_ref[...], bi_ref[...], lam_ref[...])
    pos = t * tc + lax.broadcasted_iota(jnp.int32, (tc, 1), 0)
    mult = jnp.where(pos == 0, 1.0, mult)
    a_s[...] = a
    b_s[...] = mult * gi * xc

    ridx = lax.broadcasted_iota(jnp.int32, (SUBLANES, C), 0)

    def group(gidx, h):
        r0 = pl.multiple_of(gidx * SUBLANES, SUBLANES)
        a8 = a_s[pl.ds(r0, SUBLANES), :]
        b8 = b_s[pl.ds(r0, SUBLANES), :]
        for s in (1, 2, 4):
            keep = ridx >= s
            a_sh = jnp.where(keep, pltpu.roll(a8, s, 0), 1.0)
            b_sh = jnp.where(keep, pltpu.roll(b8, s, 0), 0.0)
            b8 = a8 * b_sh + b8
            a8 = a8 * a_sh
        h8 = a8 * h + b8
        hy_ref[pl.ds(r0, SUBLANES), :] = (h8 * y_ref[pl.ds(r0, SUBLANES), :]).astype(hy_ref.dtype)
        return jnp.broadcast_to(h8[SUBLANES - 1:SUBLANES, :], h8.shape)

    h = lax.fori_loop(0, tc // SUBLANES, group, h_s[...], unroll=4)
    h_s[...] = h
    hl_ref[...] = h[0:1, :]


def rglru_prompt(xr, y, conv_w, conv_b, w_a, w_i, b_a, b_i, lam, *, tc=512):
    G, T, C = xr.shape
    bw = C // N_GATE_BLOCKS
    tc = _tile(T, tc)
    tspec = pl.BlockSpec((None, tc, bw), lambda g, n, t: (g, t, n))
    vspec = pl.BlockSpec((1, bw), lambda g, n, t: (0, n))
    wspec = pl.BlockSpec((None, bw, bw), lambda g, n, t: (n, 0, 0))
    return pl.pallas_call(
        functools.partial(_rglru_prompt_body, tc=tc),
        grid=(G, N_GATE_BLOCKS, T // tc),
        in_specs=[tspec, tspec, pl.BlockSpec((CONV_W, bw), lambda g, n, t: (0, n)), vspec,
                  wspec, wspec, vspec, vspec, vspec],
        out_specs=[tspec, pl.BlockSpec((None, 1, bw), lambda g, n, t: (g, 0, n))],
        out_shape=[jax.ShapeDtypeStruct((G, T, C), BF16), jax.ShapeDtypeStruct((G, 1, C), F32)],
        scratch_shapes=[pltpu.VMEM((tc + SUBLANES, bw), F32), pltpu.VMEM((tc, bw), F32),
                        pltpu.VMEM((tc, bw), F32), pltpu.VMEM((SUBLANES, bw), F32)],
        compiler_params=_cp(("arbitrary", "arbitrary", "arbitrary")),
        name="rglru_prompt",
    )(xr, y, conv_w, conv_b.reshape(1, C), w_a, w_i, b_a.reshape(1, C), b_i.reshape(1, C),
      lam.reshape(1, C))


def _rglru_sample_body(xr_ref, y_ref, s0_ref, s1_ref, s2_ref, h0_ref, cw_ref, cb_ref, wa_ref, wi_ref,
                       ba_ref, bi_ref, lam_ref, hy_ref, h_ref):
    xc = (cb_ref[...] + cw_ref[0:1, :] * s0_ref[...] + cw_ref[1:2, :] * s1_ref[...]
          + cw_ref[2:3, :] * s2_ref[...] + cw_ref[3:4, :] * xr_ref[...])
    a, mult, gi = _rg_gates(xc, wa_ref[...], wi_ref[...], ba_ref[...], bi_ref[...], lam_ref[...])
    h = a * h0_ref[...] + mult * gi * xc
    h_ref[...] = h
    hy_ref[...] = (h * y_ref[...]).astype(hy_ref.dtype)


def rglru_sample(xr, y, state, h0, conv_w, conv_b, w_a, w_i, b_a, b_i, lam):
    _, B, C = xr.shape
    bw = C // N_GATE_BLOCKS
    st = state.reshape(B, (CONV_W - 1) * C)
    tspec = pl.BlockSpec((None, B, bw), lambda n: (0, 0, n))
    sspec = [pl.BlockSpec((B, bw), lambda n, k=k: (0, k * N_GATE_BLOCKS + n)) for k in range(CONV_W - 1)]
    vspec = pl.BlockSpec((1, bw), lambda n: (0, n))
    wspec = pl.BlockSpec((None, bw, bw), lambda n: (n, 0, 0))
    hspec = pl.BlockSpec((B, bw), lambda n: (0, n))
    return pl.pallas_call(
        _rglru_sample_body,
        grid=(N_GATE_BLOCKS,),
        in_specs=[tspec, tspec] + sspec + [hspec, pl.BlockSpec((CONV_W, bw), lambda n: (0, n)), vspec,
                                           wspec, wspec, vspec, vspec, vspec],
        out_specs=[tspec, hspec],
        out_shape=[jax.ShapeDtypeStruct((1, B, C), BF16), jax.ShapeDtypeStruct((B, C), F32)],
        compiler_params=_cp(("arbitrary",)),
        name="rglru_sample",
    )(xr, y, st, st, st, h0, conv_w, conv_b.reshape(1, C), w_a, w_i, b_a.reshape(1, C),
      b_i.reshape(1, C), lam.reshape(1, C))


def _rope128(pe, cos, sin):
    return pe * cos + pltpu.roll(pe, QK_ROPE, 1) * sin


def _kv_body(x_ref, w_ref, g_ref, cos_ref, sin_ref, ckv_ref, kpe_ref, wbf):
    @pl.when((pl.program_id(0) == 0) & (pl.program_id(1) == 0))
    def _():
        wbf[...] = w_ref[...].astype(BF16)

    acc = jnp.dot(x_ref[...], wbf[...], preferred_element_type=F32)
    c = acc[:, :KV_LORA]
    ckv_ref[...] = c * lax.rsqrt(jnp.mean(c * c, axis=-1, keepdims=True) + EPS) * g_ref[...]
    rot = _rope128(acc[:, KV_LORA:], cos_ref[...], sin_ref[...])
    kpe_ref[...] = rot[:, :QK_ROPE]


def kv_latent(hn, w_ext, g_ckv, cos_t, sin_t, *, tm=512):
    G, T, D = hn.shape
    tm = _tile(T, tm)
    NE = w_ext.shape[-1]
    if cos_t.shape[0] == 1:
        tab = pl.BlockSpec((1, LANES), lambda g, i: (0, 0))
    else:
        tab = pl.BlockSpec((tm, LANES), lambda g, i: (i, 0))
    return pl.pallas_call(
        _kv_body, grid=(G, T // tm),
        in_specs=[pl.BlockSpec((None, tm, D), lambda g, i: (g, i, 0)),
                  pl.BlockSpec((D, NE), lambda g, i: (0, 0)),
                  pl.BlockSpec((1, KV_LORA), lambda g, i: (0, 0)), tab, tab],
        out_specs=[pl.BlockSpec((None, tm, KV_LORA), lambda g, i: (g, i, 0)),
                   pl.BlockSpec((None, tm, QK_ROPE), lambda g, i: (g, i, 0))],
        out_shape=[jax.ShapeDtypeStruct((G, T, KV_LORA), F32), jax.ShapeDtypeStruct((G, T, QK_ROPE), F32)],
        scratch_shapes=[pltpu.VMEM((D, NE), BF16)],
        compiler_params=_cp(("arbitrary", "arbitrary")), name="kv_latent",
    )(hn, w_ext, g_ckv.reshape(1, KV_LORA), cos_t, sin_t)


def _kv_heads_body(c_ref, pe_ref, wk_ref, wv_ref, gn_ref, gp_ref, k_ref, vt_ref, wk_bf, wv_bf):
    @pl.when((pl.program_id(0) == 0) & (pl.program_id(1) == 0))
    def _():
        wk_bf[...] = wk_ref[...].astype(BF16)
        wv_bf[...] = wv_ref[...].astype(BF16)

    c = c_ref[...].astype(BF16)
    pe = pe_ref[...]
    sspe = jnp.sum(pe * pe, axis=-1, keepdims=True)
    for h in range(N_HEADS):
        kh = jnp.dot(c, wk_bf[:, h * QK_NOPE:(h + 1) * QK_NOPE], preferred_element_type=F32)
        ss = jnp.sum(kh * kh, axis=-1, keepdims=True) + sspe
        inv = lax.rsqrt(ss * (1.0 / QK_HEAD) + EPS)
        k_ref[h, :, 0:QK_NOPE] = (kh * inv * gn_ref[...]).astype(k_ref.dtype)
        k_ref[h, :, QK_NOPE:QK_HEAD] = (pe * inv * gp_ref[...]).astype(k_ref.dtype)
        vt_ref[h] = lax.dot_general(wv_bf[h * V_HEAD:(h + 1) * V_HEAD, :], c, NT_DIMS,
                                    preferred_element_type=F32).astype(vt_ref.dtype)


def kv_heads(ckv, kpe, w_uk, w_uv_t, g_k, *, tm=512):
    G, T, L = ckv.shape
    tm = _tile(T, tm)
    HN = N_HEADS * QK_NOPE
    HV = N_HEADS * V_HEAD
    return pl.pallas_call(
        _kv_heads_body, grid=(G, T // tm),
        in_specs=[pl.BlockSpec((None, tm, L), lambda g, i: (g, i, 0)),
                  pl.BlockSpec((None, tm, QK_ROPE), lambda g, i: (g, i, 0)),
                  pl.BlockSpec((L, HN), lambda g, i: (0, 0)),
                  pl.BlockSpec((HV, L), lambda g, i: (0, 0)),
                  pl.BlockSpec((1, QK_NOPE), lambda g, i: (0, 0)),
                  pl.BlockSpec((1, QK_ROPE), lambda g, i: (0, 0))],
        out_specs=[pl.BlockSpec((None, N_HEADS, tm, QK_HEAD), lambda g, i: (g, 0, i, 0)),
                   pl.BlockSpec((None, N_HEADS, None, V_HEAD, tm), lambda g, i: (g, 0, i, 0, 0))],
        out_shape=[jax.ShapeDtypeStruct((G, N_HEADS, T, QK_HEAD), BF16),
                   jax.ShapeDtypeStruct((G, N_HEADS, T // tm, V_HEAD, tm), BF16)],
        scratch_shapes=[pltpu.VMEM((L, HN), BF16), pltpu.VMEM((HV, L), BF16)],
        compiler_params=_cp(("arbitrary", "arbitrary")), name="kv_heads",
    )(ckv, kpe, w_uk.reshape(L, HN), w_uv_t, g_k[:QK_NOPE].reshape(1, QK_NOPE),
      g_k[QK_NOPE:].reshape(1, QK_ROPE))


def _q_body(cq_ref, w_ref, cos_ref, sin_ref, gn_ref, gp_ref, q_ref, wbf):
    @pl.when((pl.program_id(0) == 0) & (pl.program_id(1) == 0))
    def _():
        wbf[...] = w_ref[...].astype(BF16)

    cq = cq_ref[...]
    cos = cos_ref[...]
    sin = sin_ref[...]
    hw = QK_NOPE + LANES
    for h in range(N_HEADS):
        a = jnp.dot(cq, wbf[:, h * hw:(h + 1) * hw], preferred_element_type=F32)
        nope = a[:, :QK_NOPE]
        rot = _rope128(a[:, QK_NOPE:], cos, sin)
        ss = jnp.sum(nope * nope + rot * rot, axis=-1, keepdims=True)
        inv = lax.rsqrt(ss * (1.0 / QK_HEAD) + EPS) * SCALE
        q_ref[h, :, 0:QK_NOPE] = (nope * inv * gn_ref[...]).astype(q_ref.dtype)
        q_ref[h, :, QK_NOPE:QK_HEAD] = (rot * inv * gp_ref[...])[:, :QK_ROPE].astype(q_ref.dtype)


def q_heads(cq, w_ext, cos_t, sin_t, g_q, out_dtype, *, tm=512):
    G, T, L = cq.shape
    tm = _tile(T, tm)
    NE = w_ext.shape[-1]
    if cos_t.shape[0] == 1:
        tab = pl.BlockSpec((1, LANES), lambda g, i: (0, 0))
    else:
        tab = pl.BlockSpec((tm, LANES), lambda g, i: (i, 0))
    gp = jnp.zeros((1, LANES), F32).at[0, :QK_ROPE].set(g_q[QK_NOPE:])
    return pl.pallas_call(
        _q_body, grid=(G, T // tm),
        in_specs=[pl.BlockSpec((None, tm, L), lambda g, i: (g, i, 0)),
                  pl.BlockSpec((L, NE), lambda g, i: (0, 0)), tab, tab,
                  pl.BlockSpec((1, QK_NOPE), lambda g, i: (0, 0)),
                  pl.BlockSpec((1, LANES), lambda g, i: (0, 0))],
        out_specs=pl.BlockSpec((None, N_HEADS, tm, QK_HEAD), lambda g, i: (g, 0, i, 0)),
        out_shape=jax.ShapeDtypeStruct((G, N_HEADS, T, QK_HEAD), out_dtype),
        scratch_shapes=[pltpu.VMEM((L, NE), BF16)],
        compiler_params=_cp(("arbitrary", "arbitrary")), name="q_heads",
    )(cq, w_ext, cos_t, sin_t, g_q[:QK_NOPE].reshape(1, QK_NOPE), gp)


def _flash_body(q_ref, k_ref, vt_ref, o_ref, *, tb):
    i = pl.program_id(2)
    qa = q_ref[0:tb, :]
    qb = q_ref[tb:2 * tb, :]

    def kv_block(j):
        r0 = pl.multiple_of(j * tb, tb)
        return k_ref[pl.ds(r0, tb), :], vt_ref[j]

    def scores(q, k, masked):
        s = lax.dot_general(k, q, NT_DIMS, preferred_element_type=F32)
        if masked:
            key = lax.broadcasted_iota(jnp.int32, (tb, tb), 0)
            qry = lax.broadcasted_iota(jnp.int32, (tb, tb), 1)
            s = jnp.where(key <= qry, s, -jnp.inf)
        return s

    def update(s, vt, carry):
        m, l, acc = carry
        m_new = jnp.maximum(m, jnp.max(s, axis=0, keepdims=True))
        p = jnp.exp(s - m_new)
        corr = jnp.exp(m - m_new)
        l = l * corr + jnp.sum(p, axis=0, keepdims=True)
        acc = acc * corr + jnp.dot(vt, p.astype(BF16), preferred_element_type=F32)
        return m_new, l, acc

    def both(j, carries, masked_a):
        k, vt = kv_block(j)
        sa = scores(qa, k, masked_a)
        sb = scores(qb, k, False)
        return update(sa, vt, carries[0]), update(sb, vt, carries[1])

    init = (jnp.full((1, tb), -jnp.inf, F32), jnp.zeros((1, tb), F32), jnp.zeros((V_HEAD, tb), F32))
    ca, cb = lax.fori_loop(0, 2 * i, lambda j, c: both(j, c, False), (init, init))
    ca, cb = both(2 * i, (ca, cb), True)
    k, vt = kv_block(2 * i + 1)
    cb = update(scores(qb, k, True), vt, cb)
    o_ref[0:tb, :] = jnp.transpose(ca[2] / ca[1]).astype(o_ref.dtype)
    o_ref[tb:2 * tb, :] = jnp.transpose(cb[2] / cb[1]).astype(o_ref.dtype)


def flash_prompt(q, k, vt):
    G, H, T, E = q.shape
    NB, _, tb = vt.shape[2:]
    assert T == NB * tb and T % (2 * tb) == 0
    tq = 2 * tb
    return pl.pallas_call(
        functools.partial(_flash_body, tb=tb),
        grid=(G, H, T // tq),
        in_specs=[pl.BlockSpec((None, None, tq, E), lambda g, h, i: (g, h, i, 0)),
                  pl.BlockSpec((None, None, T, E), lambda g, h, i: (g, h, 0, 0)),
                  pl.BlockSpec((None, None, NB, V_HEAD, tb), lambda g, h, i: (g, h, 0, 0, 0))],
        out_specs=pl.BlockSpec((None, tq, V_HEAD), lambda g, h, i: (g, i, h)),
        out_shape=jax.ShapeDtypeStruct((G, T, H * V_HEAD), BF16),
        compiler_params=_cp(("arbitrary", "arbitrary", "arbitrary")), name="flash_prompt",
    )(q, k, vt)


def _absorb_body(q_ref, k_ref, w_ref, gn_ref, gp_ref, qlat_ref, qpe_ref, sn_ref):
    q = q_ref[...]
    qn = (q[:, :QK_NOPE] * gn_ref[...]).astype(BF16)
    qlat_ref[...] = lax.dot_general(qn, w_ref[...].astype(BF16), NT_DIMS,
                                    preferred_element_type=F32).astype(qlat_ref.dtype)
    qpe_ref[...] = (q[:, QK_NOPE:] * gp_ref[...]).astype(qpe_ref.dtype)
    s = jnp.sum(q.astype(BF16).astype(F32) * k_ref[...].astype(F32), axis=-1, keepdims=True)
    sn_ref[...] = jnp.broadcast_to(s, sn_ref.shape)


def absorb_queries(q, k, w_uk, g_k):
    _, H, B, E = q.shape
    L = w_uk.shape[0]
    return pl.pallas_call(
        _absorb_body, grid=(H,),
        in_specs=[pl.BlockSpec((None, None, B, E), lambda h: (0, h, 0, 0)),
                  pl.BlockSpec((None, None, B, E), lambda h: (0, h, 0, 0)),
                  pl.BlockSpec((L, QK_NOPE), lambda h: (0, h)),
                  pl.BlockSpec((1, QK_NOPE), lambda h: (0, 0)),
                  pl.BlockSpec((1, QK_ROPE), lambda h: (0, 0))],
        out_specs=[pl.BlockSpec((None, B, L), lambda h: (h, 0, 0)),
                   pl.BlockSpec((None, B, QK_ROPE), lambda h: (h, 0, 0)),
                   pl.BlockSpec((None, B, LANES), lambda h: (h, 0, 0))],
        out_shape=[jax.ShapeDtypeStruct((H, B, L), BF16), jax.ShapeDtypeStruct((H, B, QK_ROPE), BF16),
                   jax.ShapeDtypeStruct((H, B, LANES), F32)],
        compiler_params=_cp(("arbitrary",)), name="absorb_queries",
    )(q, k, w_uk.reshape(L, H * QK_NOPE), g_k[:QK_NOPE].reshape(1, QK_NOPE),
      g_k[QK_NOPE:].reshape(1, QK_ROPE))


PAGES_PER_HALF = 16


def _page_copies(pt_ref, ck_hbm, kp_hbm, cbuf, kbuf, sem_c, sem_k, n, first, pp):
    copies = []
    for k in range(pp):
        page = pt_ref[n, first + k]
        copies.append(pltpu.make_async_copy(ck_hbm.at[page], cbuf.at[k], sem_c))
        copies.append(pltpu.make_async_copy(kp_hbm.at[page], kbuf.at[k], sem_k))
    return copies


def _paged_scores(waug, qpe, cbuf, kbuf):
    pp, pg, L = cbuf.shape
    HN = N_HEADS * QK_NOPE
    P = pp * pg
    cc = cbuf[...].reshape(P, L).astype(BF16)
    kpt = jnp.concatenate([kbuf[k] for k in range(pp)], axis=1)
    kt = lax.dot_general(waug[...], cc, NT_DIMS, preferred_element_type=F32)
    kn = kt[0:HN, :].reshape(N_HEADS, QK_NOPE, P)
    ssn = jnp.sum(kn * kn, axis=1)
    s_lat = kt[HN:HN + N_HEADS, :]
    sspe = jnp.sum(kpt * kpt, axis=0, keepdims=True)
    s_pe = jnp.dot(qpe, kpt.astype(BF16), preferred_element_type=F32)
    inv = lax.rsqrt((ssn + sspe) * (1.0 / QK_HEAD) + EPS)
    return (s_lat + s_pe) * inv, cc


def _paged_accumulate(s, cc, m_old, l_old, acc_old):
    m_new = jnp.maximum(m_old, jnp.max(s, axis=-1, keepdims=True))
    pexp = jnp.exp(s - m_new[:, 0:1])
    corr = jnp.exp(m_old - m_new)
    l_new = l_old * corr + jnp.sum(pexp, axis=-1, keepdims=True)
    acc_new = acc_old * corr[:, 0:1] + jnp.dot(pexp.astype(BF16), cc, preferred_element_type=F32)
    return m_new, l_new, acc_new


def _paged_body(pt_ref, qlat_ref, qpe_ref, sn_ref, cnew_ref, wt_ref, ck_hbm, kp_hbm, o_ref,
                waug, m_s, l_s, acc_s, cbuf_a, kbuf_a, cbuf_b, kbuf_b, sems, *, pp):
    n = pl.program_id(0)
    p = pl.program_id(1)
    steps = pl.num_programs(1)
    t = n * steps + p
    last_t = pl.num_programs(0) * steps - 1
    HN = N_HEADS * QK_NOPE

    def copies_a(nn, ps):
        return _page_copies(pt_ref, ck_hbm, kp_hbm, cbuf_a, kbuf_a, sems.at[0], sems.at[1], nn, 2 * ps * pp, pp)

    def copies_b(nn, ps):
        return _page_copies(pt_ref, ck_hbm, kp_hbm, cbuf_b, kbuf_b, sems.at[2], sems.at[3], nn,
                            (2 * ps + 1) * pp, pp)

    @pl.when(t == 0)
    def _():
        waug[0:HN, :] = wt_ref[...]
        for c in copies_a(n, p):
            c.start()

    for c in copies_b(n, p):
        c.start()

    @pl.when(p == 0)
    def _():
        waug[HN:HN + N_HEADS, :] = qlat_ref[...]
        m_s[...] = sn_ref[...]
        l_s[...] = jnp.ones(l_s.shape, F32)
        acc_s[...] = jnp.broadcast_to(cnew_ref[...], acc_s.shape)

    qpe = qpe_ref[...]
    state = (m_s[...], l_s[...], acc_s[...])
    for c in copies_a(n, p):
        c.wait()
    s_a, cc_a = _paged_scores(waug, qpe, cbuf_a, kbuf_a)

    nxt = jnp.minimum(t + 1, last_t)
    n_nxt = nxt // steps
    p_nxt = nxt - n_nxt * steps
    for c in copies_a(n_nxt, p_nxt):
        c.start()

    for c in copies_b(n, p):
        c.wait()
    s_b, cc_b = _paged_scores(waug, qpe, cbuf_b, kbuf_b)
    state = _paged_accumulate(s_a, cc_a, *state)
    m, l, acc = _paged_accumulate(s_b, cc_b, *state)
    m_s[...] = m
    l_s[...] = l
    acc_s[...] = acc

    @pl.when(t == last_t)
    def _():
        for c in copies_a(n_nxt, p_nxt):
            c.wait()

    @pl.when(p == steps - 1)
    def _():
        o_ref[...] = acc / l[:, 0:1]


def paged_attention(page_table, cache_ckv, cache_kpe_t, qlat, qpe, sn, c_new, w_uk_t):
    B, NP = page_table.shape
    L = cache_ckv.shape[-1]
    pp = next(c for c in (PAGES_PER_HALF, 8, 4, 2, 1) if c <= PAGES_PER_HALF and NP % (2 * c) == 0)
    HN = N_HEADS * QK_NOPE
    per_seq = lambda w: pl.BlockSpec((None, N_HEADS, w), lambda n, p, pt: (n, 0, 0))
    grid_spec = pltpu.PrefetchScalarGridSpec(
        num_scalar_prefetch=1,
        grid=(B, NP // (2 * pp)),
        in_specs=[per_seq(L), per_seq(QK_ROPE), per_seq(LANES),
                  pl.BlockSpec((None, 1, L), lambda n, p, pt: (n, 0, 0)),
                  pl.BlockSpec((HN, L), lambda n, p, pt: (0, 0)),
                  pl.BlockSpec(memory_space=pl.ANY), pl.BlockSpec(memory_space=pl.ANY)],
        out_specs=per_seq(L),
        scratch_shapes=[pltpu.VMEM((HN + N_HEADS, L), BF16), pltpu.VMEM((N_HEADS, LANES), F32),
                        pltpu.VMEM((N_HEADS, LANES), F32), pltpu.VMEM((N_HEADS, L), F32),
                        pltpu.VMEM((pp, PAGE, L), F32), pltpu.VMEM((pp, QK_ROPE, PAGE), F32),
                        pltpu.VMEM((pp, PAGE, L), F32), pltpu.VMEM((pp, QK_ROPE, PAGE), F32),
                        pltpu.SemaphoreType.DMA((4,))],
    )
    return pl.pallas_call(
        functools.partial(_paged_body, pp=pp),
        grid_spec=grid_spec,
        out_shape=jax.ShapeDtypeStruct((B, N_HEADS, L), F32),
        compiler_params=_cp(("arbitrary", "arbitrary")), name="paged_attention",
    )(page_table, qlat, qpe, sn, c_new, w_uk_t, cache_ckv, cache_kpe_t)


def _head_proj_body(x_ref, w_ref, o_ref):
    o_ref[...] = jnp.dot(x_ref[...].astype(BF16), w_ref[...].astype(BF16),
                         preferred_element_type=F32).astype(o_ref.dtype)


def head_value_proj(o_lat, w_uv):
    H, B, L = o_lat.shape
    return pl.pallas_call(
        _head_proj_body, grid=(H,),
        in_specs=[pl.BlockSpec((None, B, L), lambda h: (h, 0, 0)),
                  pl.BlockSpec((L, V_HEAD), lambda h: (0, h))],
        out_specs=pl.BlockSpec((None, B, V_HEAD), lambda h: (0, 0, h)),
        out_shape=jax.ShapeDtypeStruct((1, B, H * V_HEAD), BF16),
        compiler_params=_cp(("arbitrary",)), name="head_value_proj",
    )(o_lat, w_uv)


def _new_expert(te_ref, r):
    prev = te_ref[jnp.maximum(r - 1, 0)]
    return (r == 0) | (te_ref[r] != prev)


def _moe_up_body(te_ref, tv_ref, ts_ref, x_ref, wg_ref, wu_ref, o_ref, wg_bf, wu_bf):
    r = pl.program_id(1)

    @pl.when(_new_expert(te_ref, r))
    def _():
        wg_bf[...] = wg_ref[...].astype(BF16)
        wu_bf[...] = wu_ref[...].astype(BF16)

    @pl.when(tv_ref[r] > 0)
    def _():
        x = x_ref[...]
        a = jnp.dot(x, wg_bf[...], preferred_element_type=F32)
        u = jnp.dot(x, wu_bf[...], preferred_element_type=F32)
        o_ref[...] = (a * jax.nn.sigmoid(a) * u).astype(o_ref.dtype)

    @pl.when(tv_ref[r] == 0)
    def _():
        o_ref[...] = jnp.zeros(o_ref.shape, o_ref.dtype)


def _moe_down_body(te_ref, tv_ref, ts_ref, h_ref, wd_ref, o_ref, wd_bf):
    r = pl.program_id(1)

    @pl.when(_new_expert(te_ref, r))
    def _():
        wd_bf[...] = wd_ref[...].astype(BF16)

    @pl.when(tv_ref[r] > 0)
    def _():
        o_ref[...] = jnp.dot(h_ref[...], wd_bf[...], preferred_element_type=F32)

    @pl.when(tv_ref[r] == 0)
    def _():
        o_ref[...] = jnp.zeros(o_ref.shape, o_ref.dtype)


def moe_experts(xg, tile_expert, tile_valid, tile_src, w_gate, w_up, w_down, *, tm, tf=512, tn=512):
    R, D = xg.shape
    F = w_gate.shape[-1]
    tf = _tile(F, tf)
    tn = _tile(D, tn)
    up_spec = pltpu.PrefetchScalarGridSpec(
        num_scalar_prefetch=3, grid=(F // tf, R // tm),
        in_specs=[pl.BlockSpec((tm, D), lambda f, r, te, tv, ts: (ts[r], 0)),
                  pl.BlockSpec((None, D, tf), lambda f, r, te, tv, ts: (te[r], 0, f)),
                  pl.BlockSpec((None, D, tf), lambda f, r, te, tv, ts: (te[r], 0, f))],
        out_specs=pl.BlockSpec((tm, tf), lambda f, r, te, tv, ts: (r, f)),
        scratch_shapes=[pltpu.VMEM((D, tf), BF16), pltpu.VMEM((D, tf), BF16)])
    h = pl.pallas_call(
        _moe_up_body, grid_spec=up_spec, out_shape=jax.ShapeDtypeStruct((R, F), BF16),
        compiler_params=_cp(("arbitrary", "arbitrary")), name="moe_up",
    )(tile_expert, tile_valid, tile_src, xg, w_gate, w_up)
    down_spec = pltpu.PrefetchScalarGridSpec(
        num_scalar_prefetch=3, grid=(D // tn, R // tm),
        in_specs=[pl.BlockSpec((tm, F), lambda j, r, te, tv, ts: (ts[r], 0)),
                  pl.BlockSpec((None, F, tn), lambda j, r, te, tv, ts: (te[r], 0, j))],
        out_specs=pl.BlockSpec((tm, tn), lambda j, r, te, tv, ts: (r, j)),
        scratch_shapes=[pltpu.VMEM((F, tn), BF16)])
    return pl.pallas_call(
        _moe_down_body, grid_spec=down_spec, out_shape=jax.ShapeDtypeStruct((R, D), F32),
        compiler_params=_cp(("arbitrary", "arbitrary")), name="moe_down",
    )(tile_expert, tile_valid, tile_src, h, w_down)


def _combine_body(x_ref, g_ref, ya_ref, yb_ref, w_ref, o_ref):
    w = w_ref[...]
    f = w[:, 0:1] * ya_ref[...] + w[:, 1:2] * yb_ref[...]
    o_ref[...] = x_ref[...] + g_ref[...] * f


def moe_combine(x, gate_mod, gate_chunk, ya, yb, w, *, tm=256):
    G, T, D = x.shape
    R = gate_mod.shape[1]
    tm = _tile(T, tm)
    tspec = pl.BlockSpec((None, tm, D), lambda g, i: (g, i, 0))
    return pl.pallas_call(
        _combine_body, grid=(G, T // tm),
        in_specs=[tspec, pl.BlockSpec((None, R, D), lambda g, i: (g, 0, gate_chunk)), tspec, tspec,
                  pl.BlockSpec((None, tm, LANES), lambda g, i: (g, i, 0))],
        out_specs=tspec, out_shape=jax.ShapeDtypeStruct((G, T, D), F32),
        compiler_params=_cp(("arbitrary", "arbitrary")), name="moe_combine",
    )(x, gate_mod, ya, yb, w)


def _moe_layout(ids, tm):
    A = ids.shape[0]
    R = -(-(A + N_EXPERTS * (tm - 1)) // tm) * tm
    onehot = (ids[:, None] == jnp.arange(N_EXPERTS, dtype=jnp.int32)[None, :]).astype(jnp.int32)
    rank = jnp.cumsum(onehot, axis=0) - onehot
    counts = jnp.sum(onehot, axis=0)
    tiles = (counts + tm - 1) // tm
    tile_end = jnp.cumsum(tiles)
    row_off = (tile_end - tiles) * tm
    pos = jnp.sum(onehot * (row_off[None, :] + rank), axis=1)
    tile_id = jnp.arange(R // tm, dtype=jnp.int32)
    tile_expert = jnp.minimum(jnp.sum((tile_id[:, None] >= tile_end[None, :]).astype(jnp.int32), axis=1),
                              N_EXPERTS - 1).astype(jnp.int32)
    tile_valid = (tile_id < tile_end[-1]).astype(jnp.int32)
    tile_src = jnp.minimum(tile_id, jnp.maximum(tile_end[-1] - 1, 0)).astype(jnp.int32)
    return R, pos.astype(jnp.int32), tile_expert, tile_valid, tile_src


def _rope_tables(pos):
    half = QK_ROPE // 2
    inv = ROPE_THETA ** (-jnp.arange(half, dtype=F32) / half)
    ang = pos.astype(F32)[:, None] * inv[None, :]
    cos, sin = jnp.cos(ang), jnp.sin(ang)
    z = jnp.zeros((pos.shape[0], LANES - QK_ROPE), F32)
    return jnp.concatenate([cos, cos, z], axis=-1), jnp.concatenate([-sin, sin, z], axis=-1)


def _swap_halves(w):
    half = QK_ROPE // 2
    return jnp.concatenate([w[..., half:], w[..., :half]], axis=-1)


def kernel(x_prompt, x_sample, state_rglru_h, state_rglru_conv, cache_ckv, cache_kpe, page_table, c_prompt, c_sample, w_ada, b_ada, g_norm, rg_w_y, rg_b_y, rg_w_x, rg_b_x, rg_conv_w, rg_conv_b, rg_w_a, rg_b_a, rg_w_i, rg_b_i, rg_lam, rg_w_out, rg_b_out, kv_w_ada, kv_b_ada, kv_g_in, kv_w_dkv, kv_g_ckv, kv_w_uk, kv_w_uv, kv_g_k, mla_w_dq, mla_g_cq, mla_w_uq, mla_g_q, mla_w_o, ffn_w_gate, ffn_w_up, ffn_w_down, moe_w_router, moe_w_gate, moe_w_up, moe_w_down):
    Bp, S, D = x_prompt.shape
    Bs = x_sample.shape[0]
    past = page_table.shape[1] * PAGE
    HN = N_HEADS * QK_NOPE
    HV = N_HEADS * V_HEAD

    xs = {"p": x_prompt, "s": x_sample.reshape(1, Bs, D)}
    groups = ("p", "s")

    pad = (-(Bs + Bp)) % 16
    c_all = jnp.concatenate([c_sample, c_prompt, jnp.zeros((pad, D), F32)], axis=0)[None]

    def split_mods(m):
        return {"p": m[0, Bs:Bs + Bp].reshape(Bp, 1, -1), "s": m[:, :Bs]}

    mods = [split_mods(mm(c_all, w_ada, (l,), bias=b_ada[l], pre_silu=True, name="adaln"))
            for l in range(2)]
    mods_kv = split_mods(mm(c_all, kv_w_ada, bias=kv_b_ada, pre_silu=True, name="adaln_kv"))

    x1 = {}
    h_last = {}
    conv_last = {}
    for gname in groups:
        x = xs[gname]
        md = mods[0][gname]
        xn = norm_mod(x, g_norm[0, 0], md, 0, 1)
        y, xr = gelu_and_linear(xn, rg_w_y, rg_w_x, (0,), rg_b_y[0], rg_b_x[0])
        if gname == "p":
            hy, hl = rglru_prompt(xr, y, rg_conv_w[0], rg_conv_b[0], rg_w_a[0], rg_w_i[0], rg_b_a[0],
                                  rg_b_i[0], rg_lam[0])
            h_last[gname] = hl.reshape(1, Bp, D)
            conv_last[gname] = xr[:, S - (CONV_W - 1):, :][None]
        else:
            hy, hl = rglru_sample(xr, y, state_rglru_conv[0], state_rglru_h[0], rg_conv_w[0], rg_conv_b[0],
                                  rg_w_a[0], rg_w_i[0], rg_b_a[0], rg_b_i[0], rg_lam[0])
            h_last[gname] = hl[None]
            conv_last[gname] = jnp.concatenate([state_rglru_conv[0][:, 1:], xr.reshape(Bs, 1, D)], axis=1)[None]
        xa = mm(hy, rg_w_out, (0,), bias=rg_b_out[0], res=x, gate=md, gate_chunk=2, tn=1024, name="rg_out")
        xn = norm_mod(xa, g_norm[0, 1], md, 3, 4)
        hmid = swiglu_up(xn, ffn_w_gate, ffn_w_up, (0,), tm=1024)
        x1[gname] = mm(hmid, ffn_w_down, (0,), res=xa, gate=md, gate_chunk=5, name="ffn_down")

    w_dkv_ext = jnp.concatenate([kv_w_dkv, _swap_halves(kv_w_dkv[:, KV_LORA:])], axis=-1)
    w_uq = mla_w_uq[0]
    w_uq_ext = jnp.concatenate([w_uq, _swap_halves(w_uq[..., QK_NOPE:])], axis=-1).reshape(
        w_uq.shape[0], N_HEADS * (QK_NOPE + LANES))
    tabs = {"p": _rope_tables(jnp.arange(S)), "s": _rope_tables(jnp.full((1,), past))}
    w_uk2 = kv_w_uk.reshape(KV_LORA, HN)
    w_uv2 = kv_w_uv.reshape(KV_LORA, HV)

    ckv = {}
    kpe = {}
    xb = {}
    for gname in groups:
        x = x1[gname]
        md = mods[1][gname]
        cos_t, sin_t = tabs[gname]
        hn = norm_mod(x, kv_g_in, mods_kv[gname], 0, 1)
        ckv[gname], kpe[gname] = kv_latent(hn, w_dkv_ext, kv_g_ckv, cos_t, sin_t)
        k, vt = kv_heads(ckv[gname], kpe[gname], w_uk2, w_uv2.T, kv_g_k)
        xn = norm_mod(x, g_norm[1, 0], md, 0, 1)
        cq = mm(xn, mla_w_dq, (0,), rms_g=mla_g_cq[0], out_dtype=BF16, name="q_down")
        if gname == "p":
            q = q_heads(cq, w_uq_ext, cos_t, sin_t, mla_g_q[0], BF16)
            o = flash_prompt(q, k, vt)
        else:
            q = q_heads(cq, w_uq_ext, cos_t, sin_t, mla_g_q[0], F32)
            qlat, qpe, sn = absorb_queries(q, k, w_uk2, kv_g_k)
            o_lat = paged_attention(page_table, cache_ckv, jnp.swapaxes(cache_kpe, 1, 2), jnp.swapaxes(qlat, 0, 1),
                                    jnp.swapaxes(qpe, 0, 1), jnp.swapaxes(sn, 0, 1),
                                    ckv[gname].reshape(Bs, 1, KV_LORA), w_uk2.T.astype(BF16))
            o = head_value_proj(jnp.swapaxes(o_lat, 0, 1), w_uv2)
        xb[gname] = mm(o, mla_w_o, (0,), res=x, gate=md, gate_chunk=2, tn=1024, name="attn_out")

    xn_l, gate_l, idx_l = [], [], []
    for gname in groups:
        xn, gt, ix = norm_mod(xb[gname], g_norm[1, 1], mods[1][gname], 3, 4, router_w=moe_w_router[0])
        xn_l.append(xn.reshape(-1, D))
        gate_l.append(gt)
        idx_l.append(ix[..., :2].reshape(-1, 2))
    xn_all = jnp.concatenate(xn_l, axis=0)
    ids = jnp.concatenate(idx_l, axis=0).reshape(-1)
    tm_moe = 512
    R, pos, tile_expert, tile_valid, tile_src = _moe_layout(ids, tm_moe)
    tok_of_row = jnp.zeros((R,), jnp.int32).at[pos].set(jnp.arange(ids.shape[0], dtype=jnp.int32) // 2)
    xg = xn_all.at[tok_of_row].get(mode="promise_in_bounds")
    yg = moe_experts(xg, tile_expert, tile_valid, tile_src, moe_w_gate[0], moe_w_up[0], moe_w_down[0], tm=tm_moe)
    pos2 = pos.reshape(-1, 2)
    n_p = Bp * S
    out = {}
    for gi, gname in enumerate(groups):
        sl = slice(0, n_p) if gname == "p" else slice(n_p, n_p + Bs)
        shp = xb[gname].shape
        ya = yg.at[pos2[sl, 0]].get(mode="promise_in_bounds").reshape(shp)
        yb = yg.at[pos2[sl, 1]].get(mode="promise_in_bounds").reshape(shp)
        out[gname] = moe_combine(xb[gname], mods[1][gname], 5, ya, yb, gate_l[gi])

    return (out["p"], out["s"].reshape(Bs, 1, D),
            h_last["p"], conv_last["p"], ckv["p"], kpe["p"],
            h_last["s"], conv_last["s"], ckv["s"].reshape(Bs, 1, KV_LORA), kpe["s"].reshape(Bs, 1, QK_ROPE))
```
